```python
import math
import jax, jax.numpy as jnp
from jax import lax
import numpy as np

D_MODEL = 1024
BATCH = 8
SEQ = 4096
DEPTH = 1

CHUNK = 64
Q_BLOCK = 128
MEM_TOKENS = 256
D_MIX = D_MODEL
RWKV_HEAD = 64
RWKV_WIDTH = D_MIX // 2
RWKV_HEADS = RWKV_WIDTH // RWKV_HEAD
DECAY_LORA = 64
AAA_LORA = 64
GATE_LORA = 128
RWKV_COLS = 3 * RWKV_WIDTH + DECAY_LORA + AAA_LORA + GATE_LORA
GN_EPS = 64e-5
DIFF_WIDTH = D_MIX - RWKV_WIDTH
DIFF_HEADS = 4
DIFF_VDIM = DIFF_WIDTH // DIFF_HEADS
DIFF_QKDIM = DIFF_VDIM // 2
DIFF_COLS = 3 * DIFF_WIDTH
D_IN_TOTAL = RWKV_COLS + DIFF_COLS
MEM_HEADS = 4
MEM_HEAD_DIM = D_MODEL // MEM_HEADS
D_FF = 4 * D_MODEL
RMS_EPS = 1e-5
NEG_INF = -1e30

kernel_name = "hymba_rwkv7_diffattn_memory_block"


def rms_norm(x, w, eps=RMS_EPS):
    xf = x.astype(jnp.float32)
    y = xf * lax.rsqrt(jnp.mean(jnp.square(xf), axis=-1, keepdims=True) + eps)
    return (y * w.astype(jnp.float32)).astype(x.dtype)


def rwkv7_scan(r, decay, k, v, kk, a):
    B, S, H, N = r.shape
    xs = tuple(jnp.moveaxis(t, 1, 0) for t in (r, decay, k, v, kk, a))

    def step(state, inp):
        r_t, w_t, k_t, v_t, kk_t, a_t = inp
        sa = jnp.einsum("bhvk,bhk->bhv", state, -kk_t)
        state = (state * w_t[:, :, None, :]
                 + sa[..., None] * (kk_t * a_t)[:, :, None, :]
                 + v_t[..., None] * k_t[:, :, None, :])
        y = jnp.einsum("bhvk,bhk->bhv", state, r_t)
        return state, y

    s0 = jnp.zeros((B, H, N, N), jnp.float32)
    _, ys = lax.scan(step, s0, xs)
    return jnp.moveaxis(ys, 0, 1)


def rwkv7_group(p, mu, w0, w_dec_up, a0, a_up, g_up, k_k, k_a, r_k, lnx_w, lnx_b):
    B, S, _ = p.shape
    p_prev = jnp.pad(p, ((0, 0), (1, 0), (0, 0)))[:, :-1]
    p = p + (p_prev - p) * mu
    r, k, v, wd, ad, gd = jnp.split(
        p, [RWKV_WIDTH, 2 * RWKV_WIDTH, 3 * RWKV_WIDTH,
            3 * RWKV_WIDTH + DECAY_LORA, 3 * RWKV_WIDTH + DECAY_LORA + AAA_LORA], axis=-1)
    f32 = jnp.float32
    w_log = -jax.nn.softplus(-(w0 + jnp.tanh(wd) @ w_dec_up)) - 0.5
    decay = jnp.exp(-jnp.exp(w_log.astype(f32)))
    a = jax.nn.sigmoid((a0 + ad @ a_up).astype(f32))
    g = jax.nn.sigmoid(gd) @ g_up
    heads = lambda t: t.astype(f32).reshape(B, S, RWKV_HEADS, RWKV_HEAD)
    kk = heads(k * k_k)
    kk = kk / jnp.maximum(jnp.sqrt(jnp.sum(jnp.square(kk), -1, keepdims=True)), 1e-12)
    k_mod = k.astype(f32) * (1.0 + (a - 1.0) * k_a)
    rh, kh, vh, ah, wh = heads(r), heads(k_mod), heads(v), heads(a), heads(decay)
    y = rwkv7_scan(rh, wh, kh, vh, kk, ah)
    mean = jnp.mean(y, -1, keepdims=True)
    var = jnp.mean(jnp.square(y - mean), -1, keepdims=True)
    y = ((y - mean) * lax.rsqrt(var + GN_EPS)).reshape(B, S, RWKV_WIDTH)
    y = y * lnx_w + lnx_b
    bonus = jnp.sum(rh * kh * r_k, -1, keepdims=True) * vh
    y = y + bonus.reshape(B, S, RWKV_WIDTH)
    return (y * g).astype(p.dtype)


def diff_attn_group(p, lam_q1, lam_k1, lam_q2, lam_k2, subln_w, lam_init):
    B, S, _ = p.shape
    q, k, v = jnp.split(p, [DIFF_WIDTH, 2 * DIFF_WIDTH], axis=-1)
    q = q.reshape(B, S, DIFF_HEADS, 2, DIFF_QKDIM)
    k = k.reshape(B, S, DIFF_HEADS, 2, DIFF_QKDIM)
    v = v.reshape(B, S, DIFF_HEADS, DIFF_VDIM)
    q1, q2 = q[..., 0, :], q[..., 1, :]
    k1, k2 = k[..., 0, :], k[..., 1, :]
    f32 = jnp.float32
    lam = (jnp.exp(jnp.sum(lam_q1 * lam_k1).astype(f32))
           - jnp.exp(jnp.sum(lam_q2 * lam_k2).astype(f32)) + lam_init)
    scale = DIFF_QKDIM ** -0.5
    outs = []
    for qb in range(S // Q_BLOCK):
        qs, qe = qb * Q_BLOCK, (qb + 1) * Q_BLOCK
        mask = (jnp.arange(qe) // CHUNK)[None, :] <= (jnp.arange(qs, qe) // CHUNK)[:, None]
        s1 = jnp.einsum("bqhd,bkhd->bhqk", q1[:, qs:qe], k1[:, :qe]).astype(f32) * scale
        s2 = jnp.einsum("bqhd,bkhd->bhqk", q2[:, qs:qe], k2[:, :qe]).astype(f32) * scale
        p1 = jax.nn.softmax(jnp.where(mask, s1, NEG_INF), axis=-1)
        p2 = jax.nn.softmax(jnp.where(mask, s2, NEG_INF), axis=-1)
        attn = (p1 - lam * p2).astype(v.dtype)
        outs.append(jnp.einsum("bhqk,bkhd->bqhd", attn, v[:, :qe]))
    o = jnp.concatenate(outs, axis=1)
    o = rms_norm(o, subln_w) * (1.0 - lam_init)
    return o.reshape(B, S, DIFF_WIDTH)


def memory_cross_attn(hn, memn, w_mq, w_mk, w_mv, w_mo):
    B, S, _ = hn.shape
    M = memn.shape[1]
    q = (hn @ w_mq).reshape(B, S, MEM_HEADS, MEM_HEAD_DIM)
    k = (memn @ w_mk).reshape(B, M, MEM_HEADS, MEM_HEAD_DIM)
    v = (memn @ w_mv).reshape(B, M, MEM_HEADS, MEM_HEAD_DIM)
    s = jnp.einsum("bshd,bmhd->bhsm", q, k).astype(jnp.float32) * (MEM_HEAD_DIM ** -0.5)
    pr = jax.nn.softmax(s, axis=-1).astype(v.dtype)
    o = jnp.einsum("bhsm,bmhd->bshd", pr, v).reshape(B, S, D_MODEL)
    return o @ w_mo


def setup_inputs(seed: int = 0) -> dict:
    key = jax.random.key(seed)
    ks = jax.random.split(key, 32)
    f32 = jnp.float32
    nrm = lambda k, shape, s: jax.random.normal(k, shape, f32) * s
    L = DEPTH
    return {
        "x": nrm(ks[0], (BATCH, SEQ, D_MODEL), 1.0),
        "mem": nrm(ks[1], (BATCH, MEM_TOKENS, D_MODEL), 1.0),
        "norm_mix_w": 1.0 + nrm(ks[2], (L, D_MODEL), 0.05),
        "w_in": nrm(ks[3], (L, D_MODEL, D_IN_TOTAL), D_MODEL ** -0.5),
        "mu_shift": jax.random.uniform(ks[4], (L, RWKV_COLS), f32),
        "w_decay0": jnp.linspace(-6.0, -1.0, RWKV_WIDTH, dtype=f32)[None] + nrm(ks[5], (L, RWKV_WIDTH), 0.1),
        "w_decay_up": nrm(ks[6], (L, DECAY_LORA, RWKV_WIDTH), 0.1),
        "a0": nrm(ks[7], (L, RWKV_WIDTH), 0.1),
        "a_up": nrm(ks[8], (L, AAA_LORA, RWKV_WIDTH), AAA_LORA ** -0.5),
        "g_up": nrm(ks[9], (L, GATE_LORA, RWKV_WIDTH), GATE_LORA ** -0.5),
        "k_k": 0.85 + nrm(ks[10], (L, RWKV_WIDTH), 0.05),
        "k_a": 1.0 + nrm(ks[11], (L, RWKV_WIDTH), 0.05),
        "r_k": nrm(ks[12], (L, RWKV_HEADS, RWKV_HEAD), 0.1),
        "lnx_w": 1.0 + nrm(ks[13], (L, RWKV_WIDTH), 0.05),
        "lnx_b": nrm(ks[14], (L, RWKV_WIDTH), 0.02),
        "lam_q1": nrm(ks[15], (L, DIFF_QKDIM), 0.1),
        "lam_k1": nrm(ks[16], (L, DIFF_QKDIM), 0.1),
        "lam_q2": nrm(ks[17], (L, DIFF_QKDIM), 0.1),
        "lam_k2": nrm(ks[18], (L, DIFF_QKDIM), 0.1),
        "subln_w": 1.0 + nrm(ks[19], (L, DIFF_VDIM), 0.05),
        "w_out": nrm(ks[20], (L, D_MIX, D_MODEL), D_MIX ** -0.5),
        "norm_mem_w": 1.0 + nrm(ks[21], (L, D_MODEL), 0.05),
        "norm_src_w": 1.0 + nrm(ks[22], (L, D_MODEL), 0.05),
        "w_mq": nrm(ks[23], (L, D_MODEL, D_MODEL), D_MODEL ** -0.5),
        "w_mk": nrm(ks[24], (L, D_MODEL, D_MODEL), D_MODEL ** -0.5),
        "w_mv": nrm(ks[25], (L, D_MODEL, D_MODEL), D_MODEL ** -0.5),
        "w_mo": nrm(ks[26], (L, D_MODEL, D_MODEL), D_MODEL ** -0.5),
        "norm_mlp_w": 1.0 + nrm(ks[27], (L, D_MODEL), 0.05),
        "w_up": nrm(ks[28], (L, D_MODEL, D_FF), D_MODEL ** -0.5),
        "w_down": nrm(ks[29], (L, D_FF, D_MODEL), D_FF ** -0.5),
        "norm_final_w": 1.0 + nrm(ks[30], (D_MODEL,), 0.05),
    }


def reference(x, mem, norm_mix_w, w_in, mu_shift, w_decay0, w_decay_up, a0, a_up, g_up,
              k_k, k_a, r_k, lnx_w, lnx_b, lam_q1, lam_k1, lam_q2, lam_k2, subln_w, w_out,
              norm_mem_w, norm_src_w, w_mq, w_mk, w_mv, w_mo, norm_mlp_w, w_up, w_down,
              norm_final_w):
    h = x
    for l in range(DEPTH):
        lam_init = 0.8 - 0.6 * math.exp(-0.3 * l)
        proj = rms_norm(h, norm_mix_w[l]) @ w_in[l]
        p_rwkv, p_diff = proj[..., :RWKV_COLS], proj[..., RWKV_COLS:]
        y_rwkv = rwkv7_group(p_rwkv, mu_shift[l], w_decay0[l], w_decay_up[l], a0[l], a_up[l],
                             g_up[l], k_k[l], k_a[l], r_k[l], lnx_w[l], lnx_b[l])
        y_diff = diff_attn_group(p_diff, lam_q1[l], lam_k1[l], lam_q2[l], lam_k2[l],
                                 subln_w[l], lam_init)
        h = h + jnp.concatenate([y_rwkv, y_diff], axis=-1) @ w_out[l]
        h = h + memory_cross_attn(rms_norm(h, norm_mem_w[l]), rms_norm(mem, norm_src_w[l]),
                                  w_mq[l], w_mk[l], w_mv[l], w_mo[l])
        hn = rms_norm(h, norm_mlp_w[l])
        h = h + jnp.square(jax.nn.relu(hn @ w_up[l])) @ w_down[l]
    return rms_norm(h, norm_final_w)
```

```python
import functools
import math

import jax
import jax.numpy as jnp
from jax import lax
from jax.experimental import pallas as pl
from jax.experimental.pallas import tpu as pltpu

F32 = jnp.float32
BF16 = jnp.bfloat16

D_MODEL = 1024
CHUNK = 64
RWKV_HEAD = 64
RWKV_WIDTH = 512
RWKV_HEADS = 8
DECAY_LORA = 64
AAA_LORA = 64
GATE_LORA = 128
RWKV_COLS = 3 * RWKV_WIDTH + DECAY_LORA + AAA_LORA + GATE_LORA
GN_EPS = 64e-5
DIFF_WIDTH = 512
DIFF_HEADS = 4
DIFF_VDIM = 128
DIFF_QKDIM = 64
DIFF_COLS = 3 * DIFF_WIDTH
D_IN_TOTAL = RWKV_COLS + DIFF_COLS
MEM_HEADS = 4
MEM_HEAD_DIM = 256
D_FF = 4 * D_MODEL
RMS_EPS = 1e-5
NEG_INF = -1e30

VMEM_LIMIT_BYTES = 56 * 1024 * 1024

ROW_TILE = 512
ATTN_TILE = 256
FF_TILE = 1024


def _mm(a, b):
    return jnp.dot(a.astype(BF16), b.astype(BF16), preferred_element_type=F32)


def _mm_nt(a, b):
    return lax.dot_general(a.astype(BF16), b.astype(BF16), (((1,), (1,)), ((), ())),
                           preferred_element_type=F32)


def _mm_tn(a, b):
    return lax.dot_general(a.astype(BF16), b.astype(BF16), (((0,), (0,)), ((), ())),
                           preferred_element_type=F32)


def _rms(x, w, eps=RMS_EPS):
    return x * lax.rsqrt(jnp.mean(x * x, axis=-1, keepdims=True) + eps) * w


def _params(*sem):
    return pltpu.CompilerParams(dimension_semantics=sem, vmem_limit_bytes=VMEM_LIMIT_BYTES)


def _const_spec(shape):
    nd = len(shape)
    return pl.BlockSpec(shape, lambda *_: (0,) * nd)


def _in_proj_kernel(x_ref, nw_ref, w_ref, pr_ref, pd_ref):
    xn = _rms(x_ref[...], nw_ref[...]).astype(BF16)
    pr_ref[...] = jnp.dot(xn, w_ref[:, :RWKV_COLS], preferred_element_type=F32)
    pd_ref[...] = jnp.dot(xn, w_ref[:, RWKV_COLS:], preferred_element_type=F32).astype(BF16)


def _in_proj(x2, norm_w, w_in_bf):
    n = x2.shape[0]
    return pl.pallas_call(
        _in_proj_kernel,
        grid=(n // ROW_TILE,),
        in_specs=[pl.BlockSpec((ROW_TILE, D_MODEL), lambda i: (i, 0)),
                  _const_spec((1, D_MODEL)),
                  _const_spec((D_MODEL, D_IN_TOTAL))],
        out_specs=[pl.BlockSpec((ROW_TILE, RWKV_COLS), lambda i: (i, 0)),
                   pl.BlockSpec((ROW_TILE, DIFF_COLS), lambda i: (i, 0))],
        out_shape=[jax.ShapeDtypeStruct((n, RWKV_COLS), F32),
                   jax.ShapeDtypeStruct((n, DIFF_COLS), BF16)],
        compiler_params=_params("parallel"),
        name="in_proj",
    )(x2, norm_w, w_in_bf)


def _unit_lower_inverse(n_strict, row, col):
    eye = (row == col).astype(F32)
    n0 = jnp.where((row // 8) == (col // 8), n_strict, 0.0)
    n2 = _mm(n0, n0)
    n4 = _mm(n2, n2)
    x = _mm(_mm(eye - n0, eye + n2), eye + n4)
    size = 8
    while size < CHUNK:
        pair = (row // (2 * size)) == (col // (2 * size))
        off = jnp.where(pair & ((row // size) != (col // size)), n_strict, 0.0)
        x = x - _mm(_mm(x, off), x)
        size *= 2
    return x


def _rwkv_kernel(p_ref, mu_ref, w0_ref, wdec_ref, a0_ref, aup_ref, gup_ref, kk_ref, ka_ref,
                 rk_ref, lnw_ref, lnb_ref, seg_ref, tril_ref, o_ref, prev_ref, st_ref):
    T, W, N = CHUNK, RWKV_WIDTH, RWKV_HEAD

    @pl.when(pl.program_id(1) == 0)
    def _():
        prev_ref[...] = jnp.zeros_like(prev_ref)
        st_ref[...] = jnp.zeros_like(st_ref)

    p = p_ref[0]
    row_all = lax.broadcasted_iota(jnp.int32, p.shape, 0)
    p_prev = jnp.where(row_all == 0, prev_ref[...], pltpu.roll(p, 1, 0))
    prev_ref[...] = p[T - 1:T, :]
    xx = p + (p_prev - p) * mu_ref[...]

    r = xx[:, 0:W]
    k = xx[:, W:2 * W]
    v = xx[:, 2 * W:3 * W]
    lora_da = xx[:, 3 * W:3 * W + DECAY_LORA + AAA_LORA]
    gd = xx[:, 3 * W + DECAY_LORA + AAA_LORA:]

    z = w0_ref[...] + _mm(jnp.tanh(lora_da), wdec_ref[...])
    softplus_neg = jnp.maximum(-z, 0.0) + jnp.log(1.0 + jnp.exp(-jnp.abs(z)))
    log_decay = -jnp.exp(-softplus_neg - 0.5)
    a = jax.nn.sigmoid(a0_ref[...] + _mm(lora_da, aup_ref[...]))
    g = _mm(jax.nn.sigmoid(gd), gup_ref[...])

    seg = seg_ref[...]
    kk = k * kk_ref[...]
    kk = kk * lax.rsqrt(jnp.maximum(_mm(kk * kk, seg), 1e-24))
    k_mod = k * (1.0 + (a - 1.0) * ka_ref[...])
    b = kk * a

    ld_hi = log_decay.astype(BF16)
    ld_lo = (log_decay - ld_hi.astype(F32)).astype(BF16)
    tril = tril_ref[...]
    cl = (jnp.dot(tril, ld_hi, preferred_element_type=F32)
          + jnp.dot(tril, ld_lo, preferred_element_type=F32))
    e_incl = jnp.exp(cl)
    e_excl = jnp.exp(cl - log_decay)
    e_inv = jnp.exp(-cl)
    w_total = e_incl[T - 1:T, :]

    kd_all = kk * e_excl
    rd_all = r * e_incl
    bi_all = b * e_inv
    ki_all = k_mod * e_inv
    bw_all = bi_all * w_total
    kw_all = ki_all * w_total

    row = lax.broadcasted_iota(jnp.int32, (T, T), 0)
    col = lax.broadcasted_iota(jnp.int32, (T, T), 1)
    strict = row > col
    incl = row >= col
    eye = row == col

    ys = []
    for h in range(RWKV_HEADS):
        sl = slice(h * N, (h + 1) * N)
        kd, rd, bi, ki = kd_all[:, sl], rd_all[:, sl], bi_all[:, sl], ki_all[:, sl]
        bw, kw, vh = bw_all[:, sl], kw_all[:, sl], v[:, sl]
        a_all = _mm_nt(jnp.concatenate([kd, rd], axis=0), jnp.concatenate([bi, ki], axis=0))
        n_ab = jnp.where(strict, a_all[:T, :T], 0.0)
        a_ak = jnp.where(strict, a_all[:T, T:], 0.0)
        a_rb = jnp.where(incl, a_all[T:, :T], 0.0)
        a_rk = jnp.where(incl, a_all[T:, T:], 0.0)
        tinv = _unit_lower_inverse(n_ab, row, col)
        kdp = _mm(tinv, kd)
        ub = -_mm(tinv, _mm(a_ak, vh))
        rdp = rd - _mm(a_rb, kdp)
        yb = _mm(a_rb, ub) + _mm(a_rk, vh)
        m_h = jnp.where(eye, w_total[:, sl], 0.0) - _mm_tn(bw, kdp)
        g_h = _mm_tn(bw, ub) + _mm_tn(kw, vh)
        st = st_ref[h]
        ys.append(_mm(rdp, st) + yb)
        st_ref[h] = _mm(m_h, st) + g_h
    y = jnp.concatenate(ys, axis=1)

    inv_n = 1.0 / N
    mean = _mm(y, seg) * inv_n
    d = y - mean
    var = _mm(d * d, seg) * inv_n
    yn = d * lax.rsqrt(var + GN_EPS) * lnw_ref[...] + lnb_ref[...]
    bonus = _mm(r * k_mod * rk_ref[...], seg) * v
    o_ref[0] = ((yn + bonus) * g).astype(o_ref.dtype)


def _rwkv(p_rwkv, mu, w0, wdec_pad, a0, aup_pad, gup, k_k, k_a, r_k, lnx_w, lnx_b):
    B, S, _ = p_rwkv.shape
    W = RWKV_WIDTH
    idx = jnp.arange(W) // RWKV_HEAD
    seg = (idx[:, None] == idx[None, :]).astype(BF16)
    t = jnp.arange(CHUNK)
    tril = (t[:, None] >= t[None, :]).astype(BF16)
    vec = lambda n: _const_spec((1, n))
    return pl.pallas_call(
        _rwkv_kernel,
        grid=(B, S // CHUNK),
        in_specs=[pl.BlockSpec((1, CHUNK, RWKV_COLS), lambda b, j: (b, j, 0)),
                  vec(RWKV_COLS), vec(W), _const_spec((128, W)), vec(W), _const_spec((128, W)),
                  _const_spec((GATE_LORA, W)), vec(W), vec(W), vec(W), vec(W), vec(W),
                  _const_spec((W, W)), _const_spec((CHUNK, CHUNK))],
        out_specs=pl.BlockSpec((1, CHUNK, W), lambda b, j: (b, j, 0)),
        out_shape=jax.ShapeDtypeStruct((B, S, W), BF16),
        scratch_shapes=[pltpu.VMEM((1, RWKV_COLS), F32),
                        pltpu.VMEM((RWKV_HEADS, RWKV_HEAD, RWKV_HEAD), F32)],
        compiler_params=_params("parallel", "arbitrary"),
        name="rwkv7",
    )(p_rwkv, mu, w0, wdec_pad, a0, aup_pad, gup, k_k, k_a, r_k, lnx_w, lnx_b, seg, tril)


def _diff_attn_kernel(lam_init, q_ref, k_ref, v_ref, lq1_ref, lk1_ref, lq2_ref, lk2_ref, sw_ref,
                      o_ref):
    tq = ATTN_TILE
    i = pl.program_id(2)
    lane = lax.broadcasted_iota(jnp.int32, (tq, DIFF_VDIM), 1)
    q = q_ref[0] * (DIFF_QKDIM ** -0.5)
    q1 = jnp.where(lane < DIFF_QKDIM, q, 0).astype(BF16)
    q2 = jnp.where(lane >= DIFF_QKDIM, q, 0).astype(BF16)

    def step(kb, vb, carry, mask):
        out = []
        for qm, (m, l, acc) in zip((q1, q2), carry):
            s = lax.dot_general(qm, kb, (((1,), (1,)), ((), ())), preferred_element_type=F32)
            if mask is not None:
                s = jnp.where(mask, s, NEG_INF)
            m_new = jnp.maximum(m, jnp.max(s, axis=-1, keepdims=True))
            alpha = jnp.exp(m - m_new)
            pr = jnp.exp(s - m_new)
            l_new = alpha * l + jnp.sum(pr, axis=-1, keepdims=True)
            acc_new = alpha * acc + jnp.dot(pr.astype(BF16), vb, preferred_element_type=F32)
            out.append((m_new, l_new, acc_new))
        return tuple(out)

    def body(j, carry):
        off = pl.multiple_of(j * tq, tq)
        return step(k_ref[0, pl.ds(off, tq), :], v_ref[0, pl.ds(off, tq), :], carry, None)

    init1 = (jnp.full((tq, 1), NEG_INF, F32), jnp.zeros((tq, 1), F32), jnp.zeros((tq, DIFF_VDIM), F32))
    carry = lax.fori_loop(0, i, body, (init1, init1))

    off = pl.multiple_of(i * tq, tq)
    rq = lax.broadcasted_iota(jnp.int32, (tq, tq), 0) // CHUNK
    ck = lax.broadcasted_iota(jnp.int32, (tq, tq), 1) // CHUNK
    (m1, l1, acc1), (m2, l2, acc2) = step(k_ref[0, pl.ds(off, tq), :], v_ref[0, pl.ds(off, tq), :],
                                          carry, ck <= rq)

    lam = (jnp.exp(jnp.sum(lq1_ref[...] * lk1_ref[...], axis=-1, keepdims=True))
           - jnp.exp(jnp.sum(lq2_ref[...] * lk2_ref[...], axis=-1, keepdims=True)) + lam_init)
    o = acc1 / l1 - lam * (acc2 / l2)
    o_ref[0] = (_rms(o, sw_ref[...]) * (1.0 - lam_init)).astype(o_ref.dtype)


def _diff_attn(p_diff, lam_q1, lam_k1, lam_q2, lam_k2, subln_w, lam_init):
    B, S, _ = p_diff.shape
    H = DIFF_HEADS
    lamv = _const_spec((1, DIFF_QKDIM))
    return pl.pallas_call(
        functools.partial(_diff_attn_kernel, lam_init),
        grid=(B, H, S // ATTN_TILE),
        in_specs=[pl.BlockSpec((1, ATTN_TILE, DIFF_VDIM), lambda b, h, i: (b, i, h)),
                  pl.BlockSpec((1, S, DIFF_VDIM), lambda b, h, i: (b, 0, H + h)),
                  pl.BlockSpec((1, S, DIFF_VDIM), lambda b, h, i: (b, 0, 2 * H + h)),
                  lamv, lamv, lamv, lamv, _const_spec((1, DIFF_VDIM))],
        out_specs=pl.BlockSpec((1, ATTN_TILE, DIFF_VDIM), lambda b, h, i: (b, i, h)),
        out_shape=jax.ShapeDtypeStruct((B, S, DIFF_WIDTH), BF16),
        compiler_params=_params("parallel", "parallel", "arbitrary"),
        name="diff_attn",
    )(p_diff, p_diff, p_diff, lam_q1, lam_k1, lam_q2, lam_k2, subln_w)


def _mem_kv_kernel(m_ref, nw_ref, wk_ref, wv_ref, k_ref, v_ref):
    mn = _rms(m_ref[...], nw_ref[...]).astype(BF16)
    k_ref[...] = jnp.dot(mn, wk_ref[...], preferred_element_type=F32).astype(BF16)
    v_ref[...] = jnp.dot(mn, wv_ref[...], preferred_element_type=F32).astype(BF16)


def _mem_kv(mem2, norm_w, w_mk_bf, w_mv_bf):
    n = mem2.shape[0]
    wspec = _const_spec((D_MODEL, D_MODEL))
    rows = pl.BlockSpec((ROW_TILE, D_MODEL), lambda i: (i, 0))
    return pl.pallas_call(
        _mem_kv_kernel,
        grid=(n // ROW_TILE,),
        in_specs=[rows, _const_spec((1, D_MODEL)), wspec, wspec],
        out_specs=[rows, rows],
        out_shape=[jax.ShapeDtypeStruct((n, D_MODEL), BF16)] * 2,
        compiler_params=_params("parallel"),
        name="mem_kv",
    )(mem2, norm_w, w_mk_bf, w_mv_bf)


def _mix_mem_kernel(x_ref, yr_ref, yd_ref, wo_ref, nw_ref, wq_ref, k_ref, v_ref, wmo_ref, o_ref):
    h = (x_ref[0]
         + jnp.dot(yr_ref[0], wo_ref[:RWKV_WIDTH, :], preferred_element_type=F32)
         + jnp.dot(yd_ref[0], wo_ref[RWKV_WIDTH:, :], preferred_element_type=F32))
    hn = _rms(h, nw_ref[...]).astype(BF16)
    q = jnp.dot(hn, wq_ref[...], preferred_element_type=F32) * (MEM_HEAD_DIM ** -0.5)
    q = q.astype(BF16)
    outs = []
    for hd in range(MEM_HEADS):
        sl = slice(hd * MEM_HEAD_DIM, (hd + 1) * MEM_HEAD_DIM)
        s = lax.dot_general(q[:, sl], k_ref[0, :, sl], (((1,), (1,)), ((), ())),
                            preferred_element_type=F32)
        pr = jnp.exp(s - jnp.max(s, axis=-1, keepdims=True))
        den = jnp.sum(pr, axis=-1, keepdims=True)
        outs.append(jnp.dot(pr.astype(BF16), v_ref[0, :, sl], preferred_element_type=F32) / den)
    o = jnp.concatenate(outs, axis=1).astype(BF16)
    o_ref[0] = h + jnp.dot(o, wmo_ref[...], preferred_element_type=F32)


def _mix_mem(x, y_rwkv, y_diff, w_out_bf, norm_w, w_mq_bf, k_mem, v_mem, w_mo_bf):
    B, S, D = x.shape
    M = k_mem.shape[1]
    wspec = _const_spec((D, D))
    rows = lambda w: pl.BlockSpec((1, ROW_TILE, w), lambda b, i: (b, i, 0))
    memspec = pl.BlockSpec((1, M, D), lambda b, i: (b, 0, 0))
    return pl.pallas_call(
        _mix_mem_kernel,
        grid=(B, S // ROW_TILE),
        in_specs=[rows(D), rows(RWKV_WIDTH), rows(DIFF_WIDTH), wspec, _const_spec((1, D)), wspec,
                  memspec, memspec, wspec],
        out_specs=rows(D),
        out_shape=jax.ShapeDtypeStruct((B, S, D), F32),
        compiler_params=_params("parallel", "parallel"),
        name="mix_mem",
    )(x, y_rwkv, y_diff, w_out_bf, norm_w, w_mq_bf, k_mem, v_mem, w_mo_bf)


def _mlp_kernel(final_norm, h_ref, nw_ref, wu_ref, wd_ref, fw_ref, o_ref):
    h = h_ref[...]
    hn = _rms(h, nw_ref[...]).astype(BF16)
    acc = h
    for c in range(D_FF // FF_TILE):
        sl = slice(c * FF_TILE, (c + 1) * FF_TILE)
        u = jnp.maximum(jnp.dot(hn, wu_ref[:, sl], preferred_element_type=F32), 0.0)
        acc = acc + jnp.dot((u * u).astype(BF16), wd_ref[sl, :], preferred_element_type=F32)
    o_ref[...] = _rms(acc, fw_ref[...]) if final_norm else acc


def _mlp(h2, norm_w, w_up_bf, w_down_bf, final_w, final_norm):
    n, D = h2.shape
    rows = pl.BlockSpec((ROW_TILE, D), lambda i: (i, 0))
    return pl.pallas_call(
        functools.partial(_mlp_kernel, final_norm),
        grid=(n // ROW_TILE,),
        in_specs=[rows, _const_spec((1, D)), _const_spec((D, D_FF)), _const_spec((D_FF, D)),
                  _const_spec((1, D))],
        out_specs=rows,
        out_shape=jax.ShapeDtypeStruct((n, D), F32),
        compiler_params=_params("parallel"),
        name="mlp",
    )(h2, norm_w, w_up_bf, w_down_bf, final_w)


def kernel(x, mem, norm_mix_w, w_in, mu_shift, w_decay0, w_decay_up, a0, a_up, g_up, k_k, k_a, r_k,
           lnx_w, lnx_b, lam_q1, lam_k1, lam_q2, lam_k2, subln_w, w_out, norm_mem_w, norm_src_w,
           w_mq, w_mk, w_mv, w_mo, norm_mlp_w, w_up, w_down, norm_final_w):
    B, S, D = x.shape
    depth = norm_mix_w.shape[0]
    row = lambda t: t.reshape(1, -1)
    h = x
    for l in range(depth):
        lam_init = 0.8 - 0.6 * math.exp(-0.3 * l)
        zeros_lora = jnp.zeros((DECAY_LORA, RWKV_WIDTH), BF16)
        wdec_pad = jnp.concatenate([w_decay_up[l].astype(BF16), zeros_lora], axis=0)
        aup_pad = jnp.concatenate([zeros_lora, a_up[l].astype(BF16)], axis=0)

        p_rwkv, p_diff = _in_proj(h.reshape(B * S, D), row(norm_mix_w[l]), w_in[l].astype(BF16))
        y_rwkv = _rwkv(p_rwkv.reshape(B, S, RWKV_COLS), row(mu_shift[l]), row(w_decay0[l]),
                       wdec_pad, row(a0[l]), aup_pad, g_up[l].astype(BF16), row(k_k[l]),
                       row(k_a[l]), row(r_k[l]), row(lnx_w[l]), row(lnx_b[l]))
        y_diff = _diff_attn(p_diff.reshape(B, S, DIFF_COLS), row(lam_q1[l]), row(lam_k1[l]),
                            row(lam_q2[l]), row(lam_k2[l]), row(subln_w[l]), lam_init)
        k_mem, v_mem = _mem_kv(mem.reshape(-1, D), row(norm_src_w[l]), w_mk[l].astype(BF16),
                               w_mv[l].astype(BF16))
        M = mem.shape[1]
        h = _mix_mem(h, y_rwkv, y_diff, w_out[l].astype(BF16), row(norm_mem_w[l]),
                     w_mq[l].astype(BF16), k_mem.reshape(B, M, D), v_mem.reshape(B, M, D),
                     w_mo[l].astype(BF16))
        h = _mlp(h.reshape(B * S, D), row(norm_mlp_w[l]), w_up[l].astype(BF16),
                 w_down[l].astype(BF16), row(norm_final_w), l == depth - 1)
        h = h.reshape(B, S, D)
    return h
```

```python
import functools
import math

import jax
import jax.numpy as jnp
from jax import lax
from jax.experimental import pallas as pl
from jax.experimental.pallas import tpu as pltpu

F32 = jnp.float32
BF16 = jnp.bfloat16

D_MODEL = 1024
CHUNK = 64
RWKV_HEAD = 64
RWKV_WIDTH = 512
RWKV_HEADS = 8
DECAY_LORA = 64
AAA_LORA = 64
GATE_LORA = 128
RWKV_COLS = 3 * RWKV_WIDTH + DECAY_LORA + AAA_LORA + GATE_LORA
GN_EPS = 64e-5
DIFF_WIDTH = 512
DIFF_HEADS = 4
DIFF_VDIM = 128
DIFF_QKDIM = 64
DIFF_COLS = 3 * DIFF_WIDTH
D_IN_TOTAL = RWKV_COLS + DIFF_COLS
MEM_HEADS = 4
MEM_HEAD_DIM = 256
D_FF = 4 * D_MODEL
RMS_EPS = 1e-5
NEG_INF = -1e30

VMEM_LIMIT_BYTES = 56 * 1024 * 1024

ROW_TILE = 512
ATTN_TILE = 256
FF_TILE = 1024


def _mm(a, b):
    return jnp.dot(a.astype(BF16), b.astype(BF16), preferred_element_type=F32)


def _mm_nt(a, b):
    return lax.dot_general(a.astype(BF16), b.astype(BF16), (((1,), (1,)), ((), ())),
                           preferred_element_type=F32)


def _mm_tn(a, b):
    return lax.dot_general(a.astype(BF16), b.astype(BF16), (((0,), (0,)), ((), ())),
                           preferred_element_type=F32)


def _rms(x, w, eps=RMS_EPS):
    return x * lax.rsqrt(jnp.mean(x * x, axis=-1, keepdims=True) + eps) * w


def _params(*sem):
    return pltpu.CompilerParams(dimension_semantics=sem, vmem_limit_bytes=VMEM_LIMIT_BYTES)


def _const_spec(shape):
    nd = len(shape)
    return pl.BlockSpec(shape, lambda *_: (0,) * nd)


def _in_proj_kernel(x_ref, nw_ref, w_ref, pr_ref, pd_ref):
    xn = _rms(x_ref[...], nw_ref[...]).astype(BF16)
    pr_ref[...] = jnp.dot(xn, w_ref[:, :RWKV_COLS], preferred_element_type=F32)
    pd_ref[...] = jnp.dot(xn, w_ref[:, RWKV_COLS:], preferred_element_type=F32).astype(BF16)


def _in_proj(x2, norm_w, w_in_bf):
    n = x2.shape[0]
    return pl.pallas_call(
        _in_proj_kernel,
        grid=(n // ROW_TILE,),
        in_specs=[pl.BlockSpec((ROW_TILE, D_MODEL), lambda i: (i, 0)),
                  _const_spec((1, D_MODEL)),
                  _const_spec((D_MODEL, D_IN_TOTAL))],
        out_specs=[pl.BlockSpec((ROW_TILE, RWKV_COLS), lambda i: (i, 0)),
                   pl.BlockSpec((ROW_TILE, DIFF_COLS), lambda i: (i, 0))],
        out_shape=[jax.ShapeDtypeStruct((n, RWKV_COLS), F32),
                   jax.ShapeDtypeStruct((n, DIFF_COLS), BF16)],
        compiler_params=_params("parallel"),
        name="in_proj",
    )(x2, norm_w, w_in_bf)


GROUP_LANES = 256
HEADS_PER_GROUP = GROUP_LANES // RWKV_HEAD
N_GROUPS = RWKV_WIDTH // GROUP_LANES
RWKV_ITEMS_PER_PASS = 8


def _rwkv_tile_masks():
    r = jnp.arange(CHUNK)[:, None]
    c = (jnp.arange(GROUP_LANES) % CHUNK)[None, :]
    ms = [r > c, r >= c, r == c, (r // 8) == (c // 8)]
    size = 8
    while size < CHUNK:
        ms.append(((r // (2 * size)) == (c // (2 * size))) & ((r // size) > (c // size)))
        size *= 2
    return jnp.stack(ms).astype(F32)


def _rwkv_kernel(p_ref, mu_ref, w0_ref, wdec_ref, a0_ref, aup_ref, gup_ref, kk_ref, ka_ref,
                 rk_ref, lnw_ref, lnb_ref, seg_ref, tril_ref, tmask_ref, o_ref, prev_ref, st_ref):
    T, W, G = CHUNK, RWKV_WIDTH, GROUP_LANES
    nb = p_ref.shape[0]

    @pl.when(pl.program_id(0) == 0)
    def _():
        prev_ref[...] = jnp.zeros_like(prev_ref)
        st_ref[...] = jnp.zeros_like(st_ref)

    first_row = lax.broadcasted_iota(jnp.int32, (T, RWKV_COLS), 0) == 0
    xs = []
    for b in range(nb):
        p = p_ref[b]
        p_prev = jnp.where(first_row, prev_ref[b], pltpu.roll(p, 1, 0))
        prev_ref[b] = p[T - 1:T, :]
        xs.append(p + (p_prev - p) * mu_ref[...])
    xx = jnp.concatenate(xs, axis=0)

    r = xx[:, 0:W]
    k = xx[:, W:2 * W]
    v = xx[:, 2 * W:3 * W]
    lora_da = xx[:, 3 * W:3 * W + DECAY_LORA + AAA_LORA]
    gd = xx[:, 3 * W + DECAY_LORA + AAA_LORA:]

    seg = seg_ref[...]

    def head_sum(x):
        xb = x.astype(BF16)
        return jnp.concatenate(
            [jnp.dot(xb[:, g * G:(g + 1) * G], seg, preferred_element_type=F32)
             for g in range(N_GROUPS)], axis=1)

    z = w0_ref[...] + _mm(jnp.tanh(lora_da), wdec_ref[...])
    softplus_neg = jnp.maximum(-z, 0.0) + jnp.log(1.0 + jnp.exp(-jnp.abs(z)))
    log_decay = -jnp.exp(-softplus_neg - 0.5)
    a = jax.nn.sigmoid(a0_ref[...] + _mm(lora_da, aup_ref[...]))
    g_gate = _mm(jax.nn.sigmoid(gd), gup_ref[...])

    kk = k * kk_ref[...]
    kk = kk * lax.rsqrt(jnp.maximum(head_sum(kk * kk), 1e-24))
    k_mod = k * (1.0 + (a - 1.0) * ka_ref[...])
    b_vec = kk * a

    ld_hi = log_decay.astype(BF16)
    ld_lo = (log_decay - ld_hi.astype(F32)).astype(BF16)
    tril = tril_ref[...]
    cl = (jnp.dot(tril, ld_hi, preferred_element_type=F32)
          + jnp.dot(tril, ld_lo, preferred_element_type=F32))
    e_incl = jnp.exp(cl)
    e_inv = jnp.exp(-cl)
    kd_all = kk * jnp.exp(cl - log_decay)
    rd_all = r * e_incl
    bi_all = b_vec * e_inv
    ki_all = k_mod * e_inv

    bdm = seg
    bdm32 = bdm.astype(F32)
    eye_g = (lax.broadcasted_iota(jnp.int32, (G, G), 0)
             == lax.broadcasted_iota(jnp.int32, (G, G), 1)).astype(F32)
    m_strict, m_incl, m_eye, m_blk8 = (tmask_ref[i] for i in range(4))
    m_levels = [tmask_ref[i] for i in range(4, tmask_ref.shape[0])]

    def bd(y):
        yb = y.astype(BF16)
        return jnp.concatenate([yb] * HEADS_PER_GROUP, axis=0) * bdm

    items = [(b, g) for b in range(nb) for g in range(N_GROUPS)]
    tile = lambda x, b, g: x[b * T:(b + 1) * T, g * G:(g + 1) * G]
    y_tiles = {}
    for s0 in range(0, len(items), RWKV_ITEMS_PER_PASS):
        grp = items[s0:s0 + RWKV_ITEMS_PER_PASS]
        n_it = range(len(grp))
        kd = [tile(kd_all, b, g) for b, g in grp]
        rd = [tile(rd_all, b, g) for b, g in grp]
        bi = [tile(bi_all, b, g) for b, g in grp]
        ki = [tile(ki_all, b, g) for b, g in grp]
        vv = [tile(v, b, g) for b, g in grp]
        wt = [e_incl[b * T + T - 1:b * T + T, g * G:(g + 1) * G] for b, g in grp]

        lhs = [jnp.concatenate([kd[i], rd[i]], axis=0) for i in n_it]
        a1 = [_mm_nt(lhs[i], bd(bi[i])) for i in n_it]
        a2 = [_mm_nt(lhs[i], bd(ki[i])) for i in n_it]
        n_ab = [a1[i][:T] * m_strict for i in n_it]
        a_ak = [a2[i][:T] * m_strict for i in n_it]
        a_rb = [a1[i][T:] * m_incl for i in n_it]
        a_rk = [a2[i][T:] * m_incl for i in n_it]

        n0 = [n_ab[i] * m_blk8 for i in n_it]
        n2 = [_mm(n0[i], bd(n0[i])) for i in n_it]
        n4 = [_mm(n2[i], bd(n2[i])) for i in n_it]
        x = [_mm(m_eye - n0[i], bd(m_eye + n2[i])) for i in n_it]
        x = [_mm(x[i], bd(m_eye + n4[i])) for i in n_it]
        for m_off in m_levels:
            t1 = [_mm(x[i], bd(n_ab[i] * m_off)) for i in n_it]
            x = [x[i] - _mm(t1[i], bd(x[i])) for i in n_it]

        kdp = [_mm(x[i], bd(kd[i])) for i in n_it]
        bdv = [bd(vv[i]) for i in n_it]
        av = [_mm(a_ak[i], bdv[i]) for i in n_it]
        ub = [-_mm(x[i], bd(av[i])) for i in n_it]
        rdp = [rd[i] - _mm(a_rb[i], bd(kdp[i])) for i in n_it]
        yb = [_mm(a_rb[i], bd(ub[i])) + _mm(a_rk[i], bdv[i]) for i in n_it]
        wk = [jnp.concatenate([bi[i] * wt[i], ki[i] * wt[i]], axis=0) for i in n_it]
        m_bd = [eye_g * wt[i] - _mm_tn(wk[i][:T], kdp[i]) * bdm32 for i in n_it]
        g_bd = [_mm_tn(wk[i], jnp.concatenate([ub[i], vv[i]], axis=0)) * bdm32 for i in n_it]
        for i, (b, g) in enumerate(grp):
            st = st_ref[b * N_GROUPS + g].astype(BF16)
            y_tiles[(b, g)] = _mm(rdp[i], st) + yb[i]
            st_ref[b * N_GROUPS + g] = _mm(m_bd[i], st) + g_bd[i]

    y = jnp.concatenate(
        [jnp.concatenate([y_tiles[(b, g)] for g in range(N_GROUPS)], axis=1) for b in range(nb)],
        axis=0)

    inv_n = 1.0 / RWKV_HEAD
    mean = head_sum(y) * inv_n
    d = y - mean
    var = head_sum(d * d) * inv_n
    yn = d * lax.rsqrt(var + GN_EPS) * lnw_ref[...] + lnb_ref[...]
    bonus = head_sum(r * k_mod * rk_ref[...]) * v
    out = ((yn + bonus) * g_gate).astype(o_ref.dtype)
    for b in range(nb):
        o_ref[b] = out[b * T:(b + 1) * T]


def _rwkv(p_rwkv, mu, w0, wdec_pad, a0, aup_pad, gup, k_k, k_a, r_k, lnx_w, lnx_b):
    B, S, _ = p_rwkv.shape
    W, G = RWKV_WIDTH, GROUP_LANES
    head = jnp.arange(G) // RWKV_HEAD
    seg = (head[:, None] == head[None, :]).astype(BF16)
    t = jnp.arange(B * CHUNK)
    tril = ((t[:, None] >= t[None, :]) & (t[:, None] // CHUNK == t[None, :] // CHUNK)).astype(BF16)
    tmasks = _rwkv_tile_masks()
    vec = lambda n: _const_spec((1, n))
    return pl.pallas_call(
        _rwkv_kernel,
        grid=(S // CHUNK,),
        in_specs=[pl.BlockSpec((B, CHUNK, RWKV_COLS), lambda j: (0, j, 0)),
                  vec(RWKV_COLS), vec(W), _const_spec((128, W)), vec(W), _const_spec((128, W)),
                  _const_spec((GATE_LORA, W)), vec(W), vec(W), vec(W), vec(W), vec(W),
                  _const_spec((G, G)), _const_spec((B * CHUNK, B * CHUNK)),
                  _const_spec(tmasks.shape)],
        out_specs=pl.BlockSpec((B, CHUNK, W), lambda j: (0, j, 0)),
        out_shape=jax.ShapeDtypeStruct((B, S, W), BF16),
        scratch_shapes=[pltpu.VMEM((B, 1, RWKV_COLS), F32),
                        pltpu.VMEM((B * N_GROUPS, G, G), F32)],
        compiler_params=_params("arbitrary"),
        name="rwkv7",
    )(p_rwkv, mu, w0, wdec_pad, a0, aup_pad, gup, k_k, k_a, r_k, lnx_w, lnx_b, seg, tril, tmasks)


def _diff_attn_kernel(lam_init, q_ref, k_ref, v_ref, lq1_ref, lk1_ref, lq2_ref, lk2_ref, sw_ref,
                      o_ref):
    tq = ATTN_TILE
    i = pl.program_id(2)
    lane = lax.broadcasted_iota(jnp.int32, (tq, DIFF_VDIM), 1)
    q = q_ref[0] * (DIFF_QKDIM ** -0.5)
    q1 = jnp.where(lane < DIFF_QKDIM, q, 0).astype(BF16)
    q2 = jnp.where(lane >= DIFF_QKDIM, q, 0).astype(BF16)

    def step(kb, vb, carry, mask):
        out = []
        for qm, (m, l, acc) in zip((q1, q2), carry):
            s = lax.dot_general(qm, kb, (((1,), (1,)), ((), ())), preferred_element_type=F32)
            if mask is not None:
                s = jnp.where(mask, s, NEG_INF)
            m_new = jnp.maximum(m, jnp.max(s, axis=-1, keepdims=True))
            alpha = jnp.exp(m - m_new)
            pr = jnp.exp(s - m_new)
            l_new = alpha * l + jnp.sum(pr, axis=-1, keepdims=True)
            acc_new = alpha * acc + jnp.dot(pr.astype(BF16), vb, preferred_element_type=F32)
            out.append((m_new, l_new, acc_new))
        return tuple(out)

    def body(j, carry):
        off = pl.multiple_of(j * tq, tq)
        return step(k_ref[0, pl.ds(off, tq), :], v_ref[0, pl.ds(off, tq), :], carry, None)

    init1 = (jnp.full((tq, 1), NEG_INF, F32), jnp.zeros((tq, 1), F32), jnp.zeros((tq, DIFF_VDIM), F32))
    carry = lax.fori_loop(0, i, body, (init1, init1))

    off = pl.multiple_of(i * tq, tq)
    rq = lax.broadcasted_iota(jnp.int32, (tq, tq), 0) // CHUNK
    ck = lax.broadcasted_iota(jnp.int32, (tq, tq), 1) // CHUNK
    (m1, l1, acc1), (m2, l2, acc2) = step(k_ref[0, pl.ds(off, tq), :], v_ref[0, pl.ds(off, tq), :],
                                          carry, ck <= rq)

    lam = (jnp.exp(jnp.sum(lq1_ref[...] * lk1_ref[...], axis=-1, keepdims=True))
           - jnp.exp(jnp.sum(lq2_ref[...] * lk2_ref[...], axis=-1, keepdims=True)) + lam_init)
    o = acc1 / l1 - lam * (acc2 / l2)
    o_ref[0] = (_rms(o, sw_ref[...]) * (1.0 - lam_init)).astype(o_ref.dtype)


def _diff_attn(p_diff, lam_q1, lam_k1, lam_q2, lam_k2, subln_w, lam_init):
    B, S, _ = p_diff.shape
    H = DIFF_HEADS
    lamv = _const_spec((1, DIFF_QKDIM))
    return pl.pallas_call(
        functools.partial(_diff_attn_kernel, lam_init),
        grid=(B, H, S // ATTN_TILE),
        in_specs=[pl.BlockSpec((1, ATTN_TILE, DIFF_VDIM), lambda b, h, i: (b, i, h)),
                  pl.BlockSpec((1, S, DIFF_VDIM), lambda b, h, i: (b, 0, H + h)),
                  pl.BlockSpec((1, S, DIFF_VDIM), lambda b, h, i: (b, 0, 2 * H + h)),
                  lamv, lamv, lamv, lamv, _const_spec((1, DIFF_VDIM))],
        out_specs=pl.BlockSpec((1, ATTN_TILE, DIFF_VDIM), lambda b, h, i: (b, i, h)),
        out_shape=jax.ShapeDtypeStruct((B, S, DIFF_WIDTH), BF16),
        compiler_params=_params("parallel", "parallel", "arbitrary"),
        name="diff_attn",
    )(p_diff, p_diff, p_diff, lam_q1, lam_k1, lam_q2, lam_k2, subln_w)


def _mem_kv_kernel(m_ref, nw_ref, wk_ref, wv_ref, k_ref, v_ref):
    mn = _rms(m_ref[...], nw_ref[...]).astype(BF16)
    k_ref[...] = jnp.dot(mn, wk_ref[...], preferred_element_type=F32).astype(BF16)
    v_ref[...] = jnp.dot(mn, wv_ref[...], preferred_element_type=F32).astype(BF16)


def _mem_kv(mem2, norm_w, w_mk_bf, w_mv_bf):
    n = mem2.shape[0]
    wspec = _const_spec((D_MODEL, D_MODEL))
    rows = pl.BlockSpec((ROW_TILE, D_MODEL), lambda i: (i, 0))
    return pl.pallas_call(
        _mem_kv_kernel,
        grid=(n // ROW_TILE,),
        in_specs=[rows, _const_spec((1, D_MODEL)), wspec, wspec],
        out_specs=[rows, rows],
        out_shape=[jax.ShapeDtypeStruct((n, D_MODEL), BF16)] * 2,
        compiler_params=_params("parallel"),
        name="mem_kv",
    )(mem2, norm_w, w_mk_bf, w_mv_bf)


def _mix_mem_kernel(x_ref, yr_ref, yd_ref, wo_ref, nw_ref, wq_ref, k_ref, v_ref, wmo_ref, o_ref):
    h = (x_ref[0]
         + jnp.dot(yr_ref[0], wo_ref[:RWKV_WIDTH, :], preferred_element_type=F32)
         + jnp.dot(yd_ref[0], wo_ref[RWKV_WIDTH:, :], preferred_element_type=F32))
    hn = _rms(h, nw_ref[...]).astype(BF16)
    q = jnp.dot(hn, wq_ref[...], preferred_element_type=F32) * (MEM_HEAD_DIM ** -0.5)
    q = q.astype(BF16)
    outs = []
    for hd in range(MEM_HEADS):
        sl = slice(hd * MEM_HEAD_DIM, (hd + 1) * MEM_HEAD_DIM)
        s = lax.dot_general(q[:, sl], k_ref[0, :, sl], (((1,), (1,)), ((), ())),
                            preferred_element_type=F32)
        pr = jnp.exp(s - jnp.max(s, axis=-1, keepdims=True))
        den = jnp.sum(pr, axis=-1, keepdims=True)
        outs.append(jnp.dot(pr.astype(BF16), v_ref[0, :, sl], preferred_element_type=F32) / den)
    o = jnp.concatenate(outs, axis=1).astype(BF16)
    o_ref[0] = h + jnp.dot(o, wmo_ref[...], preferred_element_type=F32)


def _mix_mem(x, y_rwkv, y_diff, w_out_bf, norm_w, w_mq_bf, k_mem, v_mem, w_mo_bf):
    B, S, D = x.shape
    M = k_mem.shape[1]
    wspec = _const_spec((D, D))
    rows = lambda w: pl.BlockSpec((1, ROW_TILE, w), lambda b, i: (b, i, 0))
    memspec = pl.BlockSpec((1, M, D), lambda b, i: (b, 0, 0))
    return pl.pallas_call(
        _mix_mem_kernel,
        grid=(B, S // ROW_TILE),
        in_specs=[rows(D), rows(RWKV_WIDTH), rows(DIFF_WIDTH), wspec, _const_spec((1, D)), wspec,
                  memspec, memspec, wspec],
        out_specs=rows(D),
        out_shape=jax.ShapeDtypeStruct((B, S, D), F32),
        compiler_params=_params("parallel", "parallel"),
        name="mix_mem",
    )(x, y_rwkv, y_diff, w_out_bf, norm_w, w_mq_bf, k_mem, v_mem, w_mo_bf)


def _mlp_kernel(final_norm, h_ref, nw_ref, wu_ref, wd_ref, fw_ref, o_ref):
    h = h_ref[...]
    hn = _rms(h, nw_ref[...]).astype(BF16)
    acc = h
    for c in range(D_FF // FF_TILE):
        sl = slice(c * FF_TILE, (c + 1) * FF_TILE)
        u = jnp.maximum(jnp.dot(hn, wu_ref[:, sl], preferred_element_type=F32), 0.0)
        acc = acc + jnp.dot((u * u).astype(BF16), wd_ref[sl, :], preferred_element_type=F32)
    o_ref[...] = _rms(acc, fw_ref[...]) if final_norm else acc


def _mlp(h2, norm_w, w_up_bf, w_down_bf, final_w, final_norm):
    n, D = h2.shape
    rows = pl.BlockSpec((ROW_TILE, D), lambda i: (i, 0))
    return pl.pallas_call(
        functools.partial(_mlp_kernel, final_norm),
        grid=(n // ROW_TILE,),
        in_specs=[rows, _const_spec((1, D)), _const_spec((D, D_FF)), _const_spec((D_FF, D)),
                  _const_spec((1, D))],
        out_specs=rows,
        out_shape=jax.ShapeDtypeStruct((n, D), F32),
        compiler_params=_params("parallel"),
        name="mlp",
    )(h2, norm_w, w_up_bf, w_down_bf, final_w)


def kernel(x, mem, norm_mix_w, w_in, mu_shift, w_decay0, w_decay_up, a0, a_up, g_up, k_k, k_a, r_k,
           lnx_w, lnx_b, lam_q1, lam_k1, lam_q2, lam_k2, subln_w, w_out, norm_mem_w, norm_src_w,
           w_mq, w_mk, w_mv, w_mo, norm_mlp_w, w_up, w_down, norm_final_w):
    B, S, D = x.shape
    depth = norm_mix_w.shape[0]
    row = lambda t: t.reshape(1, -1)
    h = x
    for l in range(depth):
        lam_init = 0.8 - 0.6 * math.exp(-0.3 * l)
        zeros_lora = jnp.zeros((DECAY_LORA, RWKV_WIDTH), BF16)
        wdec_pad = jnp.concatenate([w_decay_up[l].astype(BF16), zeros_lora], axis=0)
        aup_pad = jnp.concatenate([zeros_lora, a_up[l].astype(BF16)], axis=0)

        p_rwkv, p_diff = _in_proj(h.reshape(B * S, D), row(norm_mix_w[l]), w_in[l].astype(BF16))
        y_rwkv = _rwkv(p_rwkv.reshape(B, S, RWKV_COLS), row(mu_shift[l]), row(w_decay0[l]),
                       wdec_pad, row(a0[l]), aup_pad, g_up[l].astype(BF16), row(k_k[l]),
                       row(k_a[l]), row(r_k[l]), row(lnx_w[l]), row(lnx_b[l]))
        y_diff = _diff_attn(p_diff.reshape(B, S, DIFF_COLS), row(lam_q1[l]), row(lam_k1[l]),
                            row(lam_q2[l]), row(lam_k2[l]), row(subln_w[l]), lam_init)
        k_mem, v_mem = _mem_kv(mem.reshape(-1, D), row(norm_src_w[l]), w_mk[l].astype(BF16),
                               w_mv[l].astype(BF16))
        M = mem.shape[1]
        h = _mix_mem(h, y_rwkv, y_diff, w_out[l].astype(BF16), row(norm_mem_w[l]),
                     w_mq[l].astype(BF16), k_mem.reshape(B, M, D), v_mem.reshape(B, M, D),
                     w_mo[l].astype(BF16))
        h = _mlp(h.reshape(B * S, D), row(norm_mlp_w[l]), w_up[l].astype(BF16),
                 w_down[l].astype(BF16), row(norm_final_w), l == depth - 1)
        h = h.reshape(B, S, D)
    return h
```

```python
import functools
import math

import jax
import jax.numpy as jnp
from jax import lax
from jax.experimental import pallas as pl
from jax.experimental.pallas import tpu as pltpu

F32 = jnp.float32
BF16 = jnp.bfloat16

D_MODEL = 1024
CHUNK = 64
RWKV_HEAD = 64
RWKV_WIDTH = 512
RWKV_HEADS = 8
DECAY_LORA = 64
AAA_LORA = 64
GATE_LORA = 128
RWKV_COLS = 3 * RWKV_WIDTH + DECAY_LORA + AAA_LORA + GATE_LORA
GN_EPS = 64e-5
DIFF_WIDTH = 512
DIFF_HEADS = 4
DIFF_VDIM = 128
DIFF_QKDIM = 64
DIFF_COLS = 3 * DIFF_WIDTH
D_IN_TOTAL = RWKV_COLS + DIFF_COLS
MEM_HEADS = 4
MEM_HEAD_DIM = 256
D_FF = 4 * D_MODEL
RMS_EPS = 1e-5
NEG_INF = -1e30
LOG2_E = 1.4426950408889634

VMEM_LIMIT_BYTES = 56 * 1024 * 1024

ROW_TILE = 512
ATTN_TILE = 512
FF_TILE = 1024


def _mm(a, b):
    return jnp.dot(a.astype(BF16), b.astype(BF16), preferred_element_type=F32)


def _mm_nt(a, b):
    return lax.dot_general(a.astype(BF16), b.astype(BF16), (((1,), (1,)), ((), ())),
                           preferred_element_type=F32)


def _mm_tn(a, b):
    return lax.dot_general(a.astype(BF16), b.astype(BF16), (((0,), (0,)), ((), ())),
                           preferred_element_type=F32)


def _rms(x, w, eps=RMS_EPS):
    return x * lax.rsqrt(jnp.mean(x * x, axis=-1, keepdims=True) + eps) * w


def _params(*sem):
    return pltpu.CompilerParams(dimension_semantics=sem, vmem_limit_bytes=VMEM_LIMIT_BYTES)


def _const_spec(shape):
    nd = len(shape)
    return pl.BlockSpec(shape, lambda *_: (0,) * nd)


def _in_proj_kernel(x_ref, nw_ref, w_ref, wvt_ref, pr_ref, qk_ref, vt_ref):
    xn = _rms(x_ref[...], nw_ref[...]).astype(BF16)
    pr_ref[...] = jnp.dot(xn, w_ref[:, :RWKV_COLS], preferred_element_type=F32)
    qk_ref[...] = jnp.dot(xn, w_ref[:, RWKV_COLS:], preferred_element_type=F32).astype(BF16)
    vt_ref[0, 0] = lax.dot_general(wvt_ref[...], xn, (((1,), (1,)), ((), ())),
                                   preferred_element_type=F32).astype(BF16)


def _in_proj(x2, norm_w, w_in_bf, w_vt_bf, batch):
    n = x2.shape[0]
    tiles_per_seq = n // batch // ATTN_TILE
    n_cols = RWKV_COLS + 2 * DIFF_WIDTH
    return pl.pallas_call(
        _in_proj_kernel,
        grid=(n // ATTN_TILE,),
        in_specs=[pl.BlockSpec((ATTN_TILE, D_MODEL), lambda i: (i, 0)),
                  _const_spec((1, D_MODEL)),
                  _const_spec((D_MODEL, n_cols)),
                  _const_spec((DIFF_WIDTH, D_MODEL))],
        out_specs=[pl.BlockSpec((ATTN_TILE, RWKV_COLS), lambda i: (i, 0)),
                   pl.BlockSpec((ATTN_TILE, 2 * DIFF_WIDTH), lambda i: (i, 0)),
                   pl.BlockSpec((1, 1, DIFF_WIDTH, ATTN_TILE),
                                lambda i: (i // tiles_per_seq, i % tiles_per_seq, 0, 0))],
        out_shape=[jax.ShapeDtypeStruct((n, RWKV_COLS), F32),
                   jax.ShapeDtypeStruct((n, 2 * DIFF_WIDTH), BF16),
                   jax.ShapeDtypeStruct((batch, tiles_per_seq, DIFF_WIDTH, ATTN_TILE), BF16)],
        compiler_params=_params("parallel"),
        name="in_proj",
    )(x2, norm_w, w_in_bf, w_vt_bf)


GROUP_LANES = 256
HEADS_PER_GROUP = GROUP_LANES // RWKV_HEAD
N_GROUPS = RWKV_WIDTH // GROUP_LANES
RWKV_ITEMS_PER_PASS = 8


def _rwkv_tile_masks():
    r = jnp.arange(CHUNK)[:, None]
    c = (jnp.arange(GROUP_LANES) % CHUNK)[None, :]
    ms = [r > c, r >= c, r == c, (r // 8) == (c // 8)]
    size = 8
    while size < CHUNK:
        ms.append(((r // (2 * size)) == (c // (2 * size))) & ((r // size) > (c // size)))
        size *= 2
    return jnp.stack(ms).astype(F32)


def _rwkv_kernel(p_ref, mu_ref, w0_ref, wdec_ref, a0_ref, aup_ref, gup_ref, kk_ref, ka_ref,
                 rk_ref, lnw_ref, lnb_ref, seg_ref, tril_ref, tmask_ref, o_ref, prev_ref, st_ref):
    T, W, G = CHUNK, RWKV_WIDTH, GROUP_LANES
    nb = p_ref.shape[0]

    @pl.when(pl.program_id(0) == 0)
    def _():
        prev_ref[...] = jnp.zeros_like(prev_ref)
        st_ref[...] = jnp.zeros_like(st_ref)

    first_row = lax.broadcasted_iota(jnp.int32, (T, RWKV_COLS), 0) == 0
    xs = []
    for b in range(nb):
        p = p_ref[b]
        p_prev = jnp.where(first_row, prev_ref[b], pltpu.roll(p, 1, 0))
        prev_ref[b] = p[T - 1:T, :]
        xs.append(p + (p_prev - p) * mu_ref[...])
    xx = jnp.concatenate(xs, axis=0)

    r = xx[:, 0:W]
    k = xx[:, W:2 * W]
    v = xx[:, 2 * W:3 * W]
    lora_da = xx[:, 3 * W:3 * W + DECAY_LORA + AAA_LORA]
    gd = xx[:, 3 * W + DECAY_LORA + AAA_LORA:]

    seg = seg_ref[...]

    def head_sum(x):
        xb = x.astype(BF16)
        return jnp.concatenate(
            [jnp.dot(xb[:, g * G:(g + 1) * G], seg, preferred_element_type=F32)
             for g in range(N_GROUPS)], axis=1)

    z = w0_ref[...] + _mm(jnp.tanh(lora_da), wdec_ref[...])
    softplus_neg = jnp.maximum(-z, 0.0) + jnp.log(1.0 + jnp.exp(-jnp.abs(z)))
    log_decay = -jnp.exp(-softplus_neg - 0.5)
    a = jax.nn.sigmoid(a0_ref[...] + _mm(lora_da, aup_ref[...]))
    g_gate = _mm(jax.nn.sigmoid(gd), gup_ref[...])

    kk = k * kk_ref[...]
    kk = kk * lax.rsqrt(jnp.maximum(head_sum(kk * kk), 1e-24))
    k_mod = k * (1.0 + (a - 1.0) * ka_ref[...])
    b_vec = kk * a

    ld_hi = log_decay.astype(BF16)
    ld_lo = (log_decay - ld_hi.astype(F32)).astype(BF16)
    tril = tril_ref[...]
    cl = (jnp.dot(tril, ld_hi, preferred_element_type=F32)
          + jnp.dot(tril, ld_lo, preferred_element_type=F32))
    e_incl = jnp.exp(cl)
    e_inv = jnp.exp(-cl)
    kd_all = kk * jnp.exp(cl - log_decay)
    rd_all = r * e_incl
    bi_all = b_vec * e_inv
    ki_all = k_mod * e_inv

    bdm = seg
    bdm32 = bdm.astype(F32)
    eye_g = (lax.broadcasted_iota(jnp.int32, (G, G), 0)
             == lax.broadcasted_iota(jnp.int32, (G, G), 1)).astype(F32)
    m_strict, m_incl, m_eye, m_blk8 = (tmask_ref[i] for i in range(4))
    m_levels = [tmask_ref[i] for i in range(4, tmask_ref.shape[0])]

    def bd(y):
        yb = y.astype(BF16)
        return jnp.concatenate([yb] * HEADS_PER_GROUP, axis=0) * bdm

    items = [(b, g) for b in range(nb) for g in range(N_GROUPS)]
    tile = lambda x, b, g: x[b * T:(b + 1) * T, g * G:(g + 1) * G]
    y_tiles = {}
    for s0 in range(0, len(items), RWKV_ITEMS_PER_PASS):
        grp = items[s0:s0 + RWKV_ITEMS_PER_PASS]
        n_it = range(len(grp))
        kd = [tile(kd_all, b, g) for b, g in grp]
        rd = [tile(rd_all, b, g) for b, g in grp]
        bi = [tile(bi_all, b, g) for b, g in grp]
        ki = [tile(ki_all, b, g) for b, g in grp]
        vv = [tile(v, b, g) for b, g in grp]
        wt = [e_incl[b * T + T - 1:b * T + T, g * G:(g + 1) * G] for b, g in grp]

        lhs = [jnp.concatenate([kd[i], rd[i]], axis=0) for i in n_it]
        a1 = [_mm_nt(lhs[i], bd(bi[i])) for i in n_it]
        a2 = [_mm_nt(lhs[i], bd(ki[i])) for i in n_it]
        n_ab = [a1[i][:T] * m_strict for i in n_it]
        a_ak = [a2[i][:T] * m_strict for i in n_it]
        a_rb = [a1[i][T:] * m_incl for i in n_it]
        a_rk = [a2[i][T:] * m_incl for i in n_it]

        n0 = [n_ab[i] * m_blk8 for i in n_it]
        n2 = [_mm(n0[i], bd(n0[i])) for i in n_it]
        n4 = [_mm(n2[i], bd(n2[i])) for i in n_it]
        x = [_mm(m_eye - n0[i], bd(m_eye + n2[i])) for i in n_it]
        x = [_mm(x[i], bd(m_eye + n4[i])) for i in n_it]
        for m_off in m_levels:
            t1 = [_mm(x[i], bd(n_ab[i] * m_off)) for i in n_it]
            x = [x[i] - _mm(t1[i], bd(x[i])) for i in n_it]

        kdp = [_mm(x[i], bd(kd[i])) for i in n_it]
        bdv = [bd(vv[i]) for i in n_it]
        av = [_mm(a_ak[i], bdv[i]) for i in n_it]
        ub = [-_mm(x[i], bd(av[i])) for i in n_it]
        rdp = [rd[i] - _mm(a_rb[i], bd(kdp[i])) for i in n_it]
        yb = [_mm(a_rb[i], bd(ub[i])) + _mm(a_rk[i], bdv[i]) for i in n_it]
        wk = [jnp.concatenate([bi[i] * wt[i], ki[i] * wt[i]], axis=0) for i in n_it]
        m_bd = [eye_g * wt[i] - _mm_tn(wk[i][:T], kdp[i]) * bdm32 for i in n_it]
        g_bd = [_mm_tn(wk[i], jnp.concatenate([ub[i], vv[i]], axis=0)) * bdm32 for i in n_it]
        for i, (b, g) in enumerate(grp):
            st = st_ref[b * N_GROUPS + g].astype(BF16)
            y_tiles[(b, g)] = _mm(rdp[i], st) + yb[i]
            st_ref[b * N_GROUPS + g] = _mm(m_bd[i], st) + g_bd[i]

    y = jnp.concatenate(
        [jnp.concatenate([y_tiles[(b, g)] for g in range(N_GROUPS)], axis=1) for b in range(nb)],
        axis=0)

    inv_n = 1.0 / RWKV_HEAD
    mean = head_sum(y) * inv_n
    d = y - mean
    var = head_sum(d * d) * inv_n
    yn = d * lax.rsqrt(var + GN_EPS) * lnw_ref[...] + lnb_ref[...]
    bonus = head_sum(r * k_mod * rk_ref[...]) * v
    out = ((yn + bonus) * g_gate).astype(o_ref.dtype)
    for b in range(nb):
        o_ref[b] = out[b * T:(b + 1) * T]


def _rwkv(p_rwkv, mu, w0, wdec_pad, a0, aup_pad, gup, k_k, k_a, r_k, lnx_w, lnx_b):
    B, S, _ = p_rwkv.shape
    W, G = RWKV_WIDTH, GROUP_LANES
    head = jnp.arange(G) // RWKV_HEAD
    seg = (head[:, None] == head[None, :]).astype(BF16)
    t = jnp.arange(B * CHUNK)
    tril = ((t[:, None] >= t[None, :]) & (t[:, None] // CHUNK == t[None, :] // CHUNK)).astype(BF16)
    tmasks = _rwkv_tile_masks()
    vec = lambda n: _const_spec((1, n))
    return pl.pallas_call(
        _rwkv_kernel,
        grid=(S // CHUNK,),
        in_specs=[pl.BlockSpec((B, CHUNK, RWKV_COLS), lambda j: (0, j, 0)),
                  vec(RWKV_COLS), vec(W), _const_spec((128, W)), vec(W), _const_spec((128, W)),
                  _const_spec((GATE_LORA, W)), vec(W), vec(W), vec(W), vec(W), vec(W),
                  _const_spec((G, G)), _const_spec((B * CHUNK, B * CHUNK)),
                  _const_spec(tmasks.shape)],
        out_specs=pl.BlockSpec((B, CHUNK, W), lambda j: (0, j, 0)),
        out_shape=jax.ShapeDtypeStruct((B, S, W), BF16),
        scratch_shapes=[pltpu.VMEM((B, 1, RWKV_COLS), F32),
                        pltpu.VMEM((B * N_GROUPS, G, G), F32)],
        compiler_params=_params("arbitrary"),
        name="rwkv7",
    )(p_rwkv, mu, w0, wdec_pad, a0, aup_pad, gup, k_k, k_a, r_k, lnx_w, lnx_b, seg, tril, tmasks)


def _diff_attn_kernel(lam_init, q_ref, k_ref, vt_ref, lq1_ref, lk1_ref, lq2_ref, lk2_ref, sw_ref,
                      o_ref, qt_ref, m_ref, l_ref, acc_ref, sa_ref, sb_ref):
    tq = tk = ATTN_TILE
    D, H = DIFF_VDIM, DIFF_HEADS
    hs = range(H)
    i = pl.program_id(1)

    d_row = lax.broadcasted_iota(jnp.int32, (D, tq), 0)
    for h in hs:
        qt = (q_ref[0, :, h * D:(h + 1) * D].astype(F32) * (DIFF_QKDIM ** -0.5 * LOG2_E)).T
        qt_ref[h] = jnp.concatenate([jnp.where(d_row < DIFF_QKDIM, qt, 0.0),
                                     jnp.where(d_row >= DIFF_QKDIM, qt, 0.0)], axis=1).astype(BF16)

    m_ref[...] = jnp.full(m_ref.shape, NEG_INF, F32)
    l_ref[...] = jnp.zeros(l_ref.shape, F32)
    acc_ref[...] = jnp.zeros(acc_ref.shape, F32)

    def produce(t, s_ref):
        off = pl.multiple_of(t * tk, tk)
        for h in hs:
            s_ref[h] = jnp.dot(k_ref[0, pl.ds(off, tk), h * D:(h + 1) * D], qt_ref[h],
                               preferred_element_type=F32)

    def consume(t, s_ref, mask):
        s = [s_ref[h] for h in hs]
        if mask is not None:
            s = [jnp.where(mask, s[h], NEG_INF) for h in hs]
        m_old = [m_ref[h] for h in hs]
        m_new = [jnp.maximum(m_old[h], jnp.max(s[h], axis=0, keepdims=True)) for h in hs]
        alpha = [jnp.exp2(m_old[h] - m_new[h]) for h in hs]
        p = [jnp.exp2(s[h] - m_new[h]) for h in hs]
        pv = [jnp.dot(vt_ref[0, t, h * D:(h + 1) * D, :], p[h].astype(BF16),
                      preferred_element_type=F32) for h in hs]
        for h in hs:
            m_ref[h] = m_new[h]
            l_ref[h] = alpha[h] * l_ref[h] + jnp.sum(p[h], axis=0, keepdims=True)
            acc_ref[h] = alpha[h] * acc_ref[h] + pv[h]

    def finish(s_ref):
        key_chunk = lax.broadcasted_iota(jnp.int32, (tk, 2 * tq), 0) // CHUNK
        qry_chunk = (lax.broadcasted_iota(jnp.int32, (tk, 2 * tq), 1) % tq) // CHUNK
        consume(i, s_ref, key_chunk <= qry_chunk)
        lam = (jnp.exp(jnp.sum(lq1_ref[...] * lk1_ref[...], axis=-1, keepdims=True))
               - jnp.exp(jnp.sum(lq2_ref[...] * lk2_ref[...], axis=-1, keepdims=True)) + lam_init)
        outs = []
        for h in hs:
            o2 = acc_ref[h] / l_ref[h]
            o = o2[:, :tq] - lam * o2[:, tq:]
            o = o * lax.rsqrt(jnp.mean(o * o, axis=0, keepdims=True) + RMS_EPS) * sw_ref[...]
            outs.append((o * (1.0 - lam_init)).T)
        o_ref[0] = jnp.concatenate(outs, axis=1).astype(o_ref.dtype)

    produce(0, sa_ref)

    def pair(pi, carry):
        t = 2 * pi
        produce(t + 1, sb_ref)
        consume(t, sa_ref, None)
        produce(t + 2, sa_ref)
        consume(t + 1, sb_ref, None)
        return carry

    lax.fori_loop(0, i // 2, pair, 0)

    @pl.when(i % 2 == 1)
    def _():
        produce(i, sb_ref)
        consume(i - 1, sa_ref, None)
        finish(sb_ref)

    @pl.when(i % 2 == 0)
    def _():
        finish(sa_ref)


def _diff_attn(qk, v_t, lam_q1, lam_k1, lam_q2, lam_k2, subln_col, lam_init):
    B, S, _ = qk.shape
    H, D, tq = DIFF_HEADS, DIFF_VDIM, ATTN_TILE
    lamv = _const_spec((1, DIFF_QKDIM))
    scores = pltpu.VMEM((H, tq, 2 * tq), F32)
    return pl.pallas_call(
        functools.partial(_diff_attn_kernel, lam_init),
        grid=(B, S // tq),
        in_specs=[pl.BlockSpec((1, tq, DIFF_WIDTH), lambda b, i: (b, i, 0)),
                  pl.BlockSpec((1, S, DIFF_WIDTH), lambda b, i: (b, 0, 1)),
                  pl.BlockSpec((1, S // tq, DIFF_WIDTH, tq), lambda b, i: (b, 0, 0, 0)),
                  lamv, lamv, lamv, lamv, _const_spec((D, 1))],
        out_specs=pl.BlockSpec((1, tq, DIFF_WIDTH), lambda b, i: (b, i, 0)),
        out_shape=jax.ShapeDtypeStruct((B, S, DIFF_WIDTH), BF16),
        scratch_shapes=[pltpu.VMEM((H, D, 2 * tq), BF16),
                        pltpu.VMEM((H, 1, 2 * tq), F32), pltpu.VMEM((H, 1, 2 * tq), F32),
                        pltpu.VMEM((H, D, 2 * tq), F32), scores, scores],
        compiler_params=_params("parallel", "arbitrary"),
        name="diff_attn",
    )(qk, qk, v_t, lam_q1, lam_k1, lam_q2, lam_k2, subln_col)


def _mem_kv_kernel(m_ref, nw_ref, wk_ref, wv_ref, k_ref, v_ref):
    mn = _rms(m_ref[...], nw_ref[...]).astype(BF16)
    k_ref[...] = jnp.dot(mn, wk_ref[...], preferred_element_type=F32).astype(BF16)
    v_ref[...] = jnp.dot(mn, wv_ref[...], preferred_element_type=F32).astype(BF16)


def _mem_kv(mem2, norm_w, w_mk_bf, w_mv_bf):
    n = mem2.shape[0]
    wspec = _const_spec((D_MODEL, D_MODEL))
    rows = pl.BlockSpec((ROW_TILE, D_MODEL), lambda i: (i, 0))
    return pl.pallas_call(
        _mem_kv_kernel,
        grid=(n // ROW_TILE,),
        in_specs=[rows, _const_spec((1, D_MODEL)), wspec, wspec],
        out_specs=[rows, rows],
        out_shape=[jax.ShapeDtypeStruct((n, D_MODEL), BF16)] * 2,
        compiler_params=_params("parallel"),
        name="mem_kv",
    )(mem2, norm_w, w_mk_bf, w_mv_bf)


def _mix_mem_kernel(x_ref, yr_ref, yd_ref, wo_ref, nw_ref, wq_ref, k_ref, v_ref, wmo_ref, o_ref):
    h = (x_ref[0]
         + jnp.dot(yr_ref[0], wo_ref[:RWKV_WIDTH, :], preferred_element_type=F32)
         + jnp.dot(yd_ref[0], wo_ref[RWKV_WIDTH:, :], preferred_element_type=F32))
    hn = _rms(h, nw_ref[...]).astype(BF16)
    q = jnp.dot(hn, wq_ref[...], preferred_element_type=F32) * (MEM_HEAD_DIM ** -0.5)
    q = q.astype(BF16)
    outs = []
    for hd in range(MEM_HEADS):
        sl = slice(hd * MEM_HEAD_DIM, (hd + 1) * MEM_HEAD_DIM)
        s = lax.dot_general(q[:, sl], k_ref[0, :, sl], (((1,), (1,)), ((), ())),
                            preferred_element_type=F32)
        pr = jnp.exp(s - jnp.max(s, axis=-1, keepdims=True))
        den = jnp.sum(pr, axis=-1, keepdims=True)
        outs.append(jnp.dot(pr.astype(BF16), v_ref[0, :, sl], preferred_element_type=F32) / den)
    o = jnp.concatenate(outs, axis=1).astype(BF16)
    o_ref[0] = h + jnp.dot(o, wmo_ref[...], preferred_element_type=F32)


def _mix_mem(x, y_rwkv, y_diff, w_out_bf, norm_w, w_mq_bf, k_mem, v_mem, w_mo_bf):
    B, S, D = x.shape
    M = k_mem.shape[1]
    wspec = _const_spec((D, D))
    rows = lambda w: pl.BlockSpec((1, ROW_TILE, w), lambda b, i: (b, i, 0))
    memspec = pl.BlockSpec((1, M, D), lambda b, i: (b, 0, 0))
    return pl.pallas_call(
        _mix_mem_kernel,
        grid=(B, S // ROW_TILE),
        in_specs=[rows(D), rows(RWKV_WIDTH), rows(DIFF_WIDTH), wspec, _const_spec((1, D)), wspec,
                  memspec, memspec, wspec],
        out_specs=rows(D),
        out_shape=jax.ShapeDtypeStruct((B, S, D), F32),
        compiler_params=_params("parallel", "parallel"),
        name="mix_mem",
    )(x, y_rwkv, y_diff, w_out_bf, norm_w, w_mq_bf, k_mem, v_mem, w_mo_bf)


def _mlp_kernel(final_norm, h_ref, nw_ref, wu_ref, wd_ref, fw_ref, o_ref):
    h = h_ref[...]
    hn = _rms(h, nw_ref[...]).astype(BF16)
    acc = h
    for c in range(D_FF // FF_TILE):
        sl = slice(c * FF_TILE, (c + 1) * FF_TILE)
        u = jnp.maximum(jnp.dot(hn, wu_ref[:, sl], preferred_element_type=F32), 0.0)
        acc = acc + jnp.dot((u * u).astype(BF16), wd_ref[sl, :], preferred_element_type=F32)
    o_ref[...] = _rms(acc, fw_ref[...]) if final_norm else acc


def _mlp(h2, norm_w, w_up_bf, w_down_bf, final_w, final_norm):
    n, D = h2.shape
    rows = pl.BlockSpec((ROW_TILE, D), lambda i: (i, 0))
    return pl.pallas_call(
        functools.partial(_mlp_kernel, final_norm),
        grid=(n // ROW_TILE,),
        in_specs=[rows, _const_spec((1, D)), _const_spec((D, D_FF)), _const_spec((D_FF, D)),
                  _const_spec((1, D))],
        out_specs=rows,
        out_shape=jax.ShapeDtypeStruct((n, D), F32),
        compiler_params=_params("parallel"),
        name="mlp",
    )(h2, norm_w, w_up_bf, w_down_bf, final_w)


def kernel(x, mem, norm_mix_w, w_in, mu_shift, w_decay0, w_decay_up, a0, a_up, g_up, k_k, k_a, r_k,
           lnx_w, lnx_b, lam_q1, lam_k1, lam_q2, lam_k2, subln_w, w_out, norm_mem_w, norm_src_w,
           w_mq, w_mk, w_mv, w_mo, norm_mlp_w, w_up, w_down, norm_final_w):
    B, S, D = x.shape
    depth = norm_mix_w.shape[0]
    row = lambda t: t.reshape(1, -1)
    h = x
    for l in range(depth):
        lam_init = 0.8 - 0.6 * math.exp(-0.3 * l)
        zeros_lora = jnp.zeros((DECAY_LORA, RWKV_WIDTH), BF16)
        wdec_pad = jnp.concatenate([w_decay_up[l].astype(BF16), zeros_lora], axis=0)
        aup_pad = jnp.concatenate([zeros_lora, a_up[l].astype(BF16)], axis=0)

        n_cols = RWKV_COLS + 2 * DIFF_WIDTH
        p_rwkv, qk, v_t = _in_proj(h.reshape(B * S, D), row(norm_mix_w[l]), w_in[l][:, :n_cols].astype(BF16),
                                   w_in[l][:, n_cols:].T.astype(BF16), B)
        y_rwkv = _rwkv(p_rwkv.reshape(B, S, RWKV_COLS), row(mu_shift[l]), row(w_decay0[l]),
                       wdec_pad, row(a0[l]), aup_pad, g_up[l].astype(BF16), row(k_k[l]),
                       row(k_a[l]), row(r_k[l]), row(lnx_w[l]), row(lnx_b[l]))
        y_diff = _diff_attn(qk.reshape(B, S, 2 * DIFF_WIDTH), v_t, row(lam_q1[l]), row(lam_k1[l]),
                            row(lam_q2[l]), row(lam_k2[l]), subln_w[l].reshape(-1, 1), lam_init)
        k_mem, v_mem = _mem_kv(mem.reshape(-1, D), row(norm_src_w[l]), w_mk[l].astype(BF16),
                               w_mv[l].astype(BF16))
        M = mem.shape[1]
        h = _mix_mem(h, y_rwkv, y_diff, w_out[l].astype(BF16), row(norm_mem_w[l]),
                     w_mq[l].astype(BF16), k_mem.reshape(B, M, D), v_mem.reshape(B, M, D),
                     w_mo[l].astype(BF16))
        h = _mlp(h.reshape(B * S, D), row(norm_mlp_w[l]), w_up[l].astype(BF16),
                 w_down[l].astype(BF16), row(norm_final_w), l == depth - 1)
        h = h.reshape(B, S, D)
    return h
```

```python
import functools
import math

import jax
import jax.numpy as jnp
from jax import lax
from jax.experimental import pallas as pl
from jax.experimental.pallas import tpu as pltpu

F32 = jnp.float32
BF16 = jnp.bfloat16

D_MODEL = 1024
CHUNK = 64
RWKV_HEAD = 64
RWKV_WIDTH = 512
RWKV_HEADS = 8
DECAY_LORA = 64
AAA_LORA = 64
GATE_LORA = 128
RWKV_COLS = 3 * RWKV_WIDTH + DECAY_LORA + AAA_LORA + GATE_LORA
GN_EPS = 64e-5
DIFF_WIDTH = 512
DIFF_HEADS = 4
DIFF_VDIM = 128
DIFF_QKDIM = 64
DIFF_COLS = 3 * DIFF_WIDTH
D_IN_TOTAL = RWKV_COLS + DIFF_COLS
MEM_HEADS = 4
MEM_HEAD_DIM = 256
D_FF = 4 * D_MODEL
RMS_EPS = 1e-5
NEG_INF = -1e30
LOG2_E = 1.4426950408889634

VMEM_LIMIT_BYTES = 56 * 1024 * 1024

ROW_TILE = 512
ATTN_TILE = 512
FF_TILE = 1024


def _mm(a, b):
    return jnp.dot(a.astype(BF16), b.astype(BF16), preferred_element_type=F32)


def _mm_nt(a, b):
    return lax.dot_general(a.astype(BF16), b.astype(BF16), (((1,), (1,)), ((), ())),
                           preferred_element_type=F32)


def _mm_tn(a, b):
    return lax.dot_general(a.astype(BF16), b.astype(BF16), (((0,), (0,)), ((), ())),
                           preferred_element_type=F32)


def _rms(x, w, eps=RMS_EPS):
    return x * lax.rsqrt(jnp.mean(x * x, axis=-1, keepdims=True) + eps) * w


def _params(*sem):
    return pltpu.CompilerParams(dimension_semantics=sem, vmem_limit_bytes=VMEM_LIMIT_BYTES)


def _const_spec(shape):
    nd = len(shape)
    return pl.BlockSpec(shape, lambda *_: (0,) * nd)


def _in_proj_kernel(x_ref, nw_ref, w_ref, wvt_ref, pr_ref, qk_ref, vt_ref):
    xn = _rms(x_ref[...], nw_ref[...]).astype(BF16)
    pr_ref[...] = jnp.dot(xn, w_ref[:, :RWKV_COLS], preferred_element_type=F32)
    qk_ref[...] = jnp.dot(xn, w_ref[:, RWKV_COLS:], preferred_element_type=F32).astype(BF16)
    vt_ref[0, 0] = lax.dot_general(wvt_ref[...], xn, (((1,), (1,)), ((), ())),
                                   preferred_element_type=F32).astype(BF16)


def _in_proj(x2, norm_w, w_in_bf, w_vt_bf, batch):
    n = x2.shape[0]
    tiles_per_seq = n // batch // ATTN_TILE
    n_cols = RWKV_COLS + 2 * DIFF_WIDTH
    return pl.pallas_call(
        _in_proj_kernel,
        grid=(n // ATTN_TILE,),
        in_specs=[pl.BlockSpec((ATTN_TILE, D_MODEL), lambda i: (i, 0)),
                  _const_spec((1, D_MODEL)),
                  _const_spec((D_MODEL, n_cols)),
                  _const_spec((DIFF_WIDTH, D_MODEL))],
        out_specs=[pl.BlockSpec((ATTN_TILE, RWKV_COLS), lambda i: (i, 0)),
                   pl.BlockSpec((ATTN_TILE, 2 * DIFF_WIDTH), lambda i: (i, 0)),
                   pl.BlockSpec((1, 1, DIFF_WIDTH, ATTN_TILE),
                                lambda i: (i // tiles_per_seq, i % tiles_per_seq, 0, 0))],
        out_shape=[jax.ShapeDtypeStruct((n, RWKV_COLS), F32),
                   jax.ShapeDtypeStruct((n, 2 * DIFF_WIDTH), BF16),
                   jax.ShapeDtypeStruct((batch, tiles_per_seq, DIFF_WIDTH, ATTN_TILE), BF16)],
        compiler_params=_params("parallel"),
        name="in_proj",
    )(x2, norm_w, w_in_bf, w_vt_bf)


GROUP_LANES = 256
HEADS_PER_GROUP = GROUP_LANES // RWKV_HEAD
N_GROUPS = RWKV_WIDTH // GROUP_LANES
RWKV_ITEMS_PER_PASS = 8


def _rwkv_tile_masks():
    r = jnp.arange(CHUNK)[:, None]
    c = (jnp.arange(GROUP_LANES) % CHUNK)[None, :]
    ms = [r > c, r >= c, r == c, (r > c) & ((r // 8) == (c // 8))]
    size = 8
    while size < CHUNK:
        ms.append(((r // (2 * size)) == (c // (2 * size))) & ((r // size) > (c // size)))
        size *= 2
    return jnp.stack(ms).astype(F32)


def _rwkv_kernel(p_ref, mu_ref, w0_ref, wdec_ref, a0_ref, aup_ref, gup_ref, kk_ref, ka_ref,
                 rk_ref, lnw_ref, lnb_ref, seg_ref, tril_ref, tmask_ref, o_ref, prev_ref, st_ref):
    T, W, G = CHUNK, RWKV_WIDTH, GROUP_LANES
    nb = p_ref.shape[0]

    @pl.when(pl.program_id(0) == 0)
    def _():
        prev_ref[...] = jnp.zeros_like(prev_ref)
        st_ref[...] = jnp.zeros_like(st_ref)

    first_row = lax.broadcasted_iota(jnp.int32, (T, RWKV_COLS), 0) == 0
    xs = []
    for b in range(nb):
        p = p_ref[b]
        p_prev = jnp.where(first_row, prev_ref[b], pltpu.roll(p, 1, 0))
        prev_ref[b] = p[T - 1:T, :]
        xs.append(p + (p_prev - p) * mu_ref[...])
    xx = jnp.concatenate(xs, axis=0)

    r = xx[:, 0:W]
    k = xx[:, W:2 * W]
    v = xx[:, 2 * W:3 * W]
    lora_da = xx[:, 3 * W:3 * W + DECAY_LORA + AAA_LORA]
    gd = xx[:, 3 * W + DECAY_LORA + AAA_LORA:]

    seg = seg_ref[...]

    def head_sum(x):
        xb = x.astype(BF16)
        return jnp.concatenate(
            [jnp.dot(xb[:, g * G:(g + 1) * G], seg, preferred_element_type=F32)
             for g in range(N_GROUPS)], axis=1)

    z = w0_ref[...] + _mm(jnp.tanh(lora_da), wdec_ref[...])
    softplus_neg = jnp.maximum(-z, 0.0) + jnp.log(1.0 + jnp.exp(-jnp.abs(z)))
    log_decay = -jnp.exp(-softplus_neg - 0.5)
    a = jax.nn.sigmoid(a0_ref[...] + _mm(lora_da, aup_ref[...]))
    g_gate = _mm(jax.nn.sigmoid(gd), gup_ref[...])

    kk = k * kk_ref[...]
    kk = kk * lax.rsqrt(jnp.maximum(head_sum(kk * kk), 1e-24))
    k_mod = k * (1.0 + (a - 1.0) * ka_ref[...])
    b_vec = kk * a

    ld_hi = log_decay.astype(BF16)
    ld_lo = (log_decay - ld_hi.astype(F32)).astype(BF16)
    tril = tril_ref[...]
    cl = (jnp.dot(tril, ld_hi, preferred_element_type=F32)
          + jnp.dot(tril, ld_lo, preferred_element_type=F32))
    e_incl = jnp.exp(cl)
    e_inv = jnp.exp(-cl)
    kd_all = kk * jnp.exp(cl - log_decay)
    rd_all = r * e_incl
    bi_all = b_vec * e_inv
    ki_all = k_mod * e_inv

    bdm = seg
    bdm32 = bdm.astype(F32)
    m_strict, m_incl, m_eye, m_blk8 = (tmask_ref[i] for i in range(4))
    m_levels = [tmask_ref[i] for i in range(4, tmask_ref.shape[0])]
    m_both = jnp.concatenate([m_strict, m_incl], axis=0)

    lane_in_vreg = lax.broadcasted_iota(jnp.int32, (T, G), 1) % 128
    half_masks = [jnp.where((lane_in_vreg // RWKV_HEAD) == j, 1.0, 0.0).astype(BF16)
                  for j in range(128 // RWKV_HEAD)]
    zero_vreg_cols = jnp.zeros((T, 128), BF16)

    def bd(y):
        yb = y.astype(BF16)
        kept = [yb * m for m in half_masks]
        rows = []
        for h in range(HEADS_PER_GROUP):
            col, j = divmod(h * RWKV_HEAD, 128)
            j //= RWKV_HEAD
            rows.append(jnp.concatenate(
                [kept[j][:, c * 128:(c + 1) * 128] if c == col else zero_vreg_cols
                 for c in range(G // 128)], axis=1))
        return jnp.concatenate(rows, axis=0)

    w_col = []
    for b in range(nb):
        col = jnp.broadcast_to(e_incl[b * T + T - 1:b * T + T, :], (128, W)).T
        w_col.append([jnp.concatenate([col[g * G:(g + 1) * G]] * (G // 128), axis=1)
                      for g in range(N_GROUPS)])

    items = [(b, g) for b in range(nb) for g in range(N_GROUPS)]
    tile = lambda x, b, g: x[b * T:(b + 1) * T, g * G:(g + 1) * G]
    y_tiles = {}
    for s0 in range(0, len(items), RWKV_ITEMS_PER_PASS):
        grp = items[s0:s0 + RWKV_ITEMS_PER_PASS]
        n_it = range(len(grp))
        kd = [tile(kd_all, b, g) for b, g in grp]
        rd = [tile(rd_all, b, g) for b, g in grp]
        bi = [tile(bi_all, b, g) for b, g in grp]
        ki = [tile(ki_all, b, g) for b, g in grp]
        vv = [tile(v, b, g) for b, g in grp]
        wt = [e_incl[b * T + T - 1:b * T + T, g * G:(g + 1) * G] for b, g in grp]

        lhs = [jnp.concatenate([kd[i], rd[i]], axis=0).astype(BF16) for i in n_it]
        a1 = [_mm_nt(lhs[i], bd(bi[i])) for i in n_it]
        a2 = [_mm_nt(lhs[i], bd(ki[i])) for i in n_it]
        a_k = [a2[i] * m_both for i in n_it]
        a_rb = [a1[i][T:] * m_incl for i in n_it]

        n0 = [a1[i][:T] * m_blk8 for i in n_it]
        n2 = [_mm(n0[i], bd(n0[i])) for i in n_it]
        n4 = [_mm(n2[i], bd(n2[i])) for i in n_it]
        x = [_mm(m_eye - n0[i], bd(m_eye + n2[i])) for i in n_it]
        x = [_mm(x[i], bd(m_eye + n4[i])) for i in n_it]
        for m_off in m_levels:
            t1 = [_mm(x[i], bd(a1[i][:T] * m_off)) for i in n_it]
            x = [x[i] - _mm(t1[i], bd(x[i])) for i in n_it]

        st = [st_ref[b * N_GROUPS + g] for b, g in grp]
        st_bf = [st[i].astype(BF16) for i in n_it]
        ks = [jnp.dot(lhs[i], st_bf[i], preferred_element_type=F32) for i in n_it]
        sv = [ks[i] + _mm(a_k[i], bd(vv[i])) for i in n_it]
        u = [-_mm(x[i], bd(sv[i][:T])) for i in n_it]
        for i, (b, g) in enumerate(grp):
            y_tiles[(b, g)] = sv[i][T:] + _mm(a_rb[i], bd(u[i]))
        wk = [jnp.concatenate([bi[i] * wt[i], ki[i] * wt[i]], axis=0) for i in n_it]
        for i, (b, g) in enumerate(grp):
            grow = _mm_tn(wk[i], jnp.concatenate([u[i], vv[i]], axis=0)) * bdm32
            st_ref[b * N_GROUPS + g] = st[i] * w_col[b][g] + grow

    y = jnp.concatenate(
        [jnp.concatenate([y_tiles[(b, g)] for g in range(N_GROUPS)], axis=1) for b in range(nb)],
        axis=0)

    inv_n = 1.0 / RWKV_HEAD
    mean = head_sum(y) * inv_n
    d = y - mean
    var = head_sum(d * d) * inv_n
    yn = d * lax.rsqrt(var + GN_EPS) * lnw_ref[...] + lnb_ref[...]
    bonus = head_sum(r * k_mod * rk_ref[...]) * v
    out = ((yn + bonus) * g_gate).astype(o_ref.dtype)
    for b in range(nb):
        o_ref[b] = out[b * T:(b + 1) * T]


def _rwkv(p_rwkv, mu, w0, wdec_pad, a0, aup_pad, gup, k_k, k_a, r_k, lnx_w, lnx_b):
    B, S, _ = p_rwkv.shape
    W, G = RWKV_WIDTH, GROUP_LANES
    head = jnp.arange(G) // RWKV_HEAD
    seg = (head[:, None] == head[None, :]).astype(BF16)
    t = jnp.arange(B * CHUNK)
    tril = ((t[:, None] >= t[None, :]) & (t[:, None] // CHUNK == t[None, :] // CHUNK)).astype(BF16)
    tmasks = _rwkv_tile_masks()
    vec = lambda n: _const_spec((1, n))
    return pl.pallas_call(
        _rwkv_kernel,
        grid=(S // CHUNK,),
        in_specs=[pl.BlockSpec((B, CHUNK, RWKV_COLS), lambda j: (0, j, 0)),
                  vec(RWKV_COLS), vec(W), _const_spec((128, W)), vec(W), _const_spec((128, W)),
                  _const_spec((GATE_LORA, W)), vec(W), vec(W), vec(W), vec(W), vec(W),
                  _const_spec((G, G)), _const_spec((B * CHUNK, B * CHUNK)),
                  _const_spec(tmasks.shape)],
        out_specs=pl.BlockSpec((B, CHUNK, W), lambda j: (0, j, 0)),
        out_shape=jax.ShapeDtypeStruct((B, S, W), BF16),
        scratch_shapes=[pltpu.VMEM((B, 1, RWKV_COLS), F32),
                        pltpu.VMEM((B * N_GROUPS, G, G), F32)],
        compiler_params=_params("arbitrary"),
        name="rwkv7",
    )(p_rwkv, mu, w0, wdec_pad, a0, aup_pad, gup, k_k, k_a, r_k, lnx_w, lnx_b, seg, tril, tmasks)


def _diff_attn_kernel(lam_init, q_ref, k_ref, vt_ref, lq1_ref, lk1_ref, lq2_ref, lk2_ref, sw_ref,
                      o_ref, qt_ref, m_ref, l_ref, acc_ref, sa_ref, sb_ref):
    tq = tk = ATTN_TILE
    D, H = DIFF_VDIM, DIFF_HEADS
    hs = range(H)
    i = pl.program_id(1)

    d_row = lax.broadcasted_iota(jnp.int32, (D, tq), 0)
    for h in hs:
        qt = (q_ref[0, :, h * D:(h + 1) * D].astype(F32) * (DIFF_QKDIM ** -0.5 * LOG2_E)).T
        qt_ref[h] = jnp.concatenate([jnp.where(d_row < DIFF_QKDIM, qt, 0.0),
                                     jnp.where(d_row >= DIFF_QKDIM, qt, 0.0)], axis=1).astype(BF16)

    m_ref[...] = jnp.full(m_ref.shape, NEG_INF, F32)
    l_ref[...] = jnp.zeros(l_ref.shape, F32)
    acc_ref[...] = jnp.zeros(acc_ref.shape, F32)

    def produce(t, s_ref):
        off = pl.multiple_of(t * tk, tk)
        for h in hs:
            s_ref[h] = jnp.dot(k_ref[0, pl.ds(off, tk), h * D:(h + 1) * D], qt_ref[h],
                               preferred_element_type=F32)

    def consume(t, s_ref, mask):
        s = [s_ref[h] for h in hs]
        if mask is not None:
            s = [jnp.where(mask, s[h], NEG_INF) for h in hs]
        m_old = [m_ref[h] for h in hs]
        m_new = [jnp.maximum(m_old[h], jnp.max(s[h], axis=0, keepdims=True)) for h in hs]
        alpha = [jnp.exp2(m_old[h] - m_new[h]) for h in hs]
        p = [jnp.exp2(s[h] - m_new[h]) for h in hs]
        pv = [jnp.dot(vt_ref[0, t, h * D:(h + 1) * D, :], p[h].astype(BF16),
                      preferred_element_type=F32) for h in hs]
        for h in hs:
            m_ref[h] = m_new[h]
            l_ref[h] = alpha[h] * l_ref[h] + jnp.sum(p[h], axis=0, keepdims=True)
            acc_ref[h] = alpha[h] * acc_ref[h] + pv[h]

    def finish(s_ref):
        key_chunk = lax.broadcasted_iota(jnp.int32, (tk, 2 * tq), 0) // CHUNK
        qry_chunk = (lax.broadcasted_iota(jnp.int32, (tk, 2 * tq), 1) % tq) // CHUNK
        consume(i, s_ref, key_chunk <= qry_chunk)
        lam = (jnp.exp(jnp.sum(lq1_ref[...] * lk1_ref[...], axis=-1, keepdims=True))
               - jnp.exp(jnp.sum(lq2_ref[...] * lk2_ref[...], axis=-1, keepdims=True)) + lam_init)
        outs = []
        for h in hs:
            o2 = acc_ref[h] / l_ref[h]
            o = o2[:, :tq] - lam * o2[:, tq:]
            o = o * lax.rsqrt(jnp.mean(o * o, axis=0, keepdims=True) + RMS_EPS) * sw_ref[...]
            outs.append((o * (1.0 - lam_init)).T)
        o_ref[0] = jnp.concatenate(outs, axis=1).astype(o_ref.dtype)

    produce(0, sa_ref)

    def pair(pi, carry):
        t = 2 * pi
        produce(t + 1, sb_ref)
        consume(t, sa_ref, None)
        produce(t + 2, sa_ref)
        consume(t + 1, sb_ref, None)
        return carry

    lax.fori_loop(0, i // 2, pair, 0)

    @pl.when(i % 2 == 1)
    def _():
        produce(i, sb_ref)
        consume(i - 1, sa_ref, None)
        finish(sb_ref)

    @pl.when(i % 2 == 0)
    def _():
        finish(sa_ref)


def _diff_attn(qk, v_t, lam_q1, lam_k1, lam_q2, lam_k2, subln_col, lam_init):
    B, S, _ = qk.shape
    H, D, tq = DIFF_HEADS, DIFF_VDIM, ATTN_TILE
    lamv = _const_spec((1, DIFF_QKDIM))
    scores = pltpu.VMEM((H, tq, 2 * tq), F32)
    return pl.pallas_call(
        functools.partial(_diff_attn_kernel, lam_init),
        grid=(B, S // tq),
        in_specs=[pl.BlockSpec((1, tq, DIFF_WIDTH), lambda b, i: (b, i, 0)),
                  pl.BlockSpec((1, S, DIFF_WIDTH), lambda b, i: (b, 0, 1)),
                  pl.BlockSpec((1, S // tq, DIFF_WIDTH, tq), lambda b, i: (b, 0, 0, 0)),
                  lamv, lamv, lamv, lamv, _const_spec((D, 1))],
        out_specs=pl.BlockSpec((1, tq, DIFF_WIDTH), lambda b, i: (b, i, 0)),
        out_shape=jax.ShapeDtypeStruct((B, S, DIFF_WIDTH), BF16),
        scratch_shapes=[pltpu.VMEM((H, D, 2 * tq), BF16),
                        pltpu.VMEM((H, 1, 2 * tq), F32), pltpu.VMEM((H, 1, 2 * tq), F32),
                        pltpu.VMEM((H, D, 2 * tq), F32), scores, scores],
        compiler_params=_params("parallel", "arbitrary"),
        name="diff_attn",
    )(qk, qk, v_t, lam_q1, lam_k1, lam_q2, lam_k2, subln_col)


def _mem_kv_kernel(m_ref, nw_ref, wk_ref, wv_ref, k_ref, v_ref):
    mn = _rms(m_ref[...], nw_ref[...]).astype(BF16)
    k_ref[...] = jnp.dot(mn, wk_ref[...], preferred_element_type=F32).astype(BF16)
    v_ref[...] = jnp.dot(mn, wv_ref[...], preferred_element_type=F32).astype(BF16)


def _mem_kv(mem2, norm_w, w_mk_bf, w_mv_bf):
    n = mem2.shape[0]
    wspec = _const_spec((D_MODEL, D_MODEL))
    rows = pl.BlockSpec((ROW_TILE, D_MODEL), lambda i: (i, 0))
    return pl.pallas_call(
        _mem_kv_kernel,
        grid=(n // ROW_TILE,),
        in_specs=[rows, _const_spec((1, D_MODEL)), wspec, wspec],
        out_specs=[rows, rows],
        out_shape=[jax.ShapeDtypeStruct((n, D_MODEL), BF16)] * 2,
        compiler_params=_params("parallel"),
        name="mem_kv",
    )(mem2, norm_w, w_mk_bf, w_mv_bf)


def _mix_mem_kernel(x_ref, yr_ref, yd_ref, wo_ref, nw_ref, wq_ref, k_ref, v_ref, wmo_ref, o_ref):
    h = (x_ref[0]
         + jnp.dot(yr_ref[0], wo_ref[:RWKV_WIDTH, :], preferred_element_type=F32)
         + jnp.dot(yd_ref[0], wo_ref[RWKV_WIDTH:, :], preferred_element_type=F32))
    hn = _rms(h, nw_ref[...]).astype(BF16)
    q = jnp.dot(hn, wq_ref[...], preferred_element_type=F32) * (MEM_HEAD_DIM ** -0.5)
    q = q.astype(BF16)
    outs = []
    for hd in range(MEM_HEADS):
        sl = slice(hd * MEM_HEAD_DIM, (hd + 1) * MEM_HEAD_DIM)
        s = lax.dot_general(q[:, sl], k_ref[0, :, sl], (((1,), (1,)), ((), ())),
                            preferred_element_type=F32)
        pr = jnp.exp(s - jnp.max(s, axis=-1, keepdims=True))
        den = jnp.sum(pr, axis=-1, keepdims=True)
        outs.append(jnp.dot(pr.astype(BF16), v_ref[0, :, sl], preferred_element_type=F32) / den)
    o = jnp.concatenate(outs, axis=1).astype(BF16)
    o_ref[0] = h + jnp.dot(o, wmo_ref[...], preferred_element_type=F32)


def _mix_mem(x, y_rwkv, y_diff, w_out_bf, norm_w, w_mq_bf, k_mem, v_mem, w_mo_bf):
    B, S, D = x.shape
    M = k_mem.shape[1]
    wspec = _const_spec((D, D))
    rows = lambda w: pl.BlockSpec((1, ROW_TILE, w), lambda b, i: (b, i, 0))
    memspec = pl.BlockSpec((1, M, D), lambda b, i: (b, 0, 0))
    return pl.pallas_call(
        _mix_mem_kernel,
        grid=(B, S // ROW_TILE),
        in_specs=[rows(D), rows(RWKV_WIDTH), rows(DIFF_WIDTH), wspec, _const_spec((1, D)), wspec,
                  memspec, memspec, wspec],
        out_specs=rows(D),
        out_shape=jax.ShapeDtypeStruct((B, S, D), F32),
        compiler_params=_params("parallel", "parallel"),
        name="mix_mem",
    )(x, y_rwkv, y_diff, w_out_bf, norm_w, w_mq_bf, k_mem, v_mem, w_mo_bf)


def _mlp_kernel(final_norm, h_ref, nw_ref, wu_ref, wd_ref, fw_ref, o_ref):
    h = h_ref[...]
    hn = _rms(h, nw_ref[...]).astype(BF16)
    acc = h
    for c in range(D_FF // FF_TILE):
        sl = slice(c * FF_TILE, (c + 1) * FF_TILE)
        u = jnp.maximum(jnp.dot(hn, wu_ref[:, sl], preferred_element_type=F32), 0.0)
        acc = acc + jnp.dot((u * u).astype(BF16), wd_ref[sl, :], preferred_element_type=F32)
    o_ref[...] = _rms(acc, fw_ref[...]) if final_norm else acc


def _mlp(h2, norm_w, w_up_bf, w_down_bf, final_w, final_norm):
    n, D = h2.shape
    rows = pl.BlockSpec((ROW_TILE, D), lambda i: (i, 0))
    return pl.pallas_call(
        functools.partial(_mlp_kernel, final_norm),
        grid=(n // ROW_TILE,),
        in_specs=[rows, _const_spec((1, D)), _const_spec((D, D_FF)), _const_spec((D_FF, D)),
                  _const_spec((1, D))],
        out_specs=rows,
        out_shape=jax.ShapeDtypeStruct((n, D), F32),
        compiler_params=_params("parallel"),
        name="mlp",
    )(h2, norm_w, w_up_bf, w_down_bf, final_w)


def kernel(x, mem, norm_mix_w, w_in, mu_shift, w_decay0, w_decay_up, a0, a_up, g_up, k_k, k_a, r_k,
           lnx_w, lnx_b, lam_q1, lam_k1, lam_q2, lam_k2, subln_w, w_out, norm_mem_w, norm_src_w,
           w_mq, w_mk, w_mv, w_mo, norm_mlp_w, w_up, w_down, norm_final_w):
    B, S, D = x.shape
    depth = norm_mix_w.shape[0]
    row = lambda t: t.reshape(1, -1)
    h = x
    for l in range(depth):
        lam_init = 0.8 - 0.6 * math.exp(-0.3 * l)
        zeros_lora = jnp.zeros((DECAY_LORA, RWKV_WIDTH), BF16)
        wdec_pad = jnp.concatenate([w_decay_up[l].astype(BF16), zeros_lora], axis=0)
        aup_pad = jnp.concatenate([zeros_lora, a_up[l].astype(BF16)], axis=0)

        n_cols = RWKV_COLS + 2 * DIFF_WIDTH
        p_rwkv, qk, v_t = _in_proj(h.reshape(B * S, D), row(norm_mix_w[l]), w_in[l][:, :n_cols].astype(BF16),
                                   w_in[l][:, n_cols:].T.astype(BF16), B)
        y_rwkv = _rwkv(p_rwkv.reshape(B, S, RWKV_COLS), row(mu_shift[l]), row(w_decay0[l]),
                       wdec_pad, row(a0[l]), aup_pad, g_up[l].astype(BF16), row(k_k[l]),
                       row(k_a[l]), row(r_k[l]), row(lnx_w[l]), row(lnx_b[l]))
        y_diff = _diff_attn(qk.reshape(B, S, 2 * DIFF_WIDTH), v_t, row(lam_q1[l]), row(lam_k1[l]),
                            row(lam_q2[l]), row(lam_k2[l]), subln_w[l].reshape(-1, 1), lam_init)
        k_mem, v_mem = _mem_kv(mem.reshape(-1, D), row(norm_src_w[l]), w_mk[l].astype(BF16),
                               w_mv[l].astype(BF16))
        M = mem.shape[1]
        h = _mix_mem(h, y_rwkv, y_diff, w_out[l].astype(BF16), row(norm_mem_w[l]),
                     w_mq[l].astype(BF16), k_mem.reshape(B, M, D), v_mem.reshape(B, M, D),
                     w_mo[l].astype(BF16))
        h = _mlp(h.reshape(B * S, D), row(norm_mlp_w[l]), w_up[l].astype(BF16),
                 w_down[l].astype(BF16), row(norm_final_w), l == depth - 1)
        h = h.reshape(B, S, D)
    return h
```

```python
import functools
import math

import jax
import jax.numpy as jnp
from jax import lax
from jax.experimental import pallas as pl
from jax.experimental.pallas import tpu as pltpu

F32 = jnp.float32
BF16 = jnp.bfloat16

D_MODEL = 1024
CHUNK = 64
RWKV_HEAD = 64
RWKV_WIDTH = 512
RWKV_HEADS = 8
DECAY_LORA = 64
AAA_LORA = 64
GATE_LORA = 128
RWKV_COLS = 3 * RWKV_WIDTH + DECAY_LORA + AAA_LORA + GATE_LORA
GN_EPS = 64e-5
DIFF_WIDTH = 512
DIFF_HEADS = 4
DIFF_VDIM = 128
DIFF_QKDIM = 64
DIFF_COLS = 3 * DIFF_WIDTH
D_IN_TOTAL = RWKV_COLS + DIFF_COLS
MEM_HEADS = 4
MEM_HEAD_DIM = 256
D_FF = 4 * D_MODEL
RMS_EPS = 1e-5
NEG_INF = -1e30
LOG2_E = 1.4426950408889634

VMEM_LIMIT_BYTES = 56 * 1024 * 1024

ROW_TILE = 1024
MLP_ROW_TILE = 512
ATTN_TILE = 512
FF_TILE = 1024


def _mm(a, b):
    return jnp.dot(a.astype(BF16), b.astype(BF16), preferred_element_type=F32)


def _mm_nt(a, b):
    return lax.dot_general(a.astype(BF16), b.astype(BF16), (((1,), (1,)), ((), ())),
                           preferred_element_type=F32)


def _mm_tn(a, b):
    return lax.dot_general(a.astype(BF16), b.astype(BF16), (((0,), (0,)), ((), ())),
                           preferred_element_type=F32)


def _rms(x, w, eps=RMS_EPS):
    return x * lax.rsqrt(jnp.mean(x * x, axis=-1, keepdims=True) + eps) * w


def _params(*sem, flags=None):
    return pltpu.CompilerParams(dimension_semantics=sem, vmem_limit_bytes=VMEM_LIMIT_BYTES,
                                flags=flags)


def _interleave(*gens):
    live = list(gens)
    while live:
        for gen in list(live):
            if next(gen, StopIteration) is StopIteration:
                live.remove(gen)


def _const_spec(shape):
    nd = len(shape)
    return pl.BlockSpec(shape, lambda *_: (0,) * nd, pipeline_mode=pl.Buffered(1))


K_COL0 = RWKV_COLS + DIFF_WIDTH


def _in_proj_kernel(tiles_per_seq, x_ref, nw_ref, w_ref, wqvt_ref, mu_ref, pr_ref, k_ref, qt_ref, vt_ref,
                    carry_ref):
    i = pl.program_id(0)
    rows = x_ref.shape[0]

    @pl.when(i == 0)
    def _():
        carry_ref[...] = jnp.zeros_like(carry_ref)

    xn = _rms(x_ref[...], nw_ref[...]).astype(BF16)
    p = jnp.dot(xn, w_ref[:, :RWKV_COLS], preferred_element_type=F32)
    last_prev = jnp.where(i % tiles_per_seq == 0, 0.0, carry_ref[...])
    first_row = lax.broadcasted_iota(jnp.int32, p.shape, 0) == 0
    p_prev = jnp.where(first_row, last_prev, pltpu.roll(p, 1, 0))
    carry_ref[...] = p[rows - 1:rows, :]
    pr_ref[...] = p + (p_prev - p) * mu_ref[...]
    k_ref[...] = jnp.dot(xn, w_ref[:, K_COL0:], preferred_element_type=F32).astype(BF16)
    qv_t = lax.dot_general(wqvt_ref[...], xn, (((1,), (1,)), ((), ())), preferred_element_type=F32)
    qt_ref[0, 0] = (qv_t[:DIFF_WIDTH] * (DIFF_QKDIM ** -0.5 * LOG2_E)).astype(BF16)
    vt_ref[0, 0] = qv_t[DIFF_WIDTH:].astype(BF16)


def _in_proj(x2, norm_w, w_in_bf, w_qvt_bf, mu, batch):
    n = x2.shape[0]
    tiles_per_seq = n // batch // ATTN_TILE
    n_cols = RWKV_COLS + 2 * DIFF_WIDTH
    t_spec = pl.BlockSpec((1, 1, DIFF_WIDTH, ATTN_TILE),
                          lambda i: (i // tiles_per_seq, i % tiles_per_seq, 0, 0))
    t_shape = jax.ShapeDtypeStruct((batch, tiles_per_seq, DIFF_WIDTH, ATTN_TILE), BF16)
    return pl.pallas_call(
        functools.partial(_in_proj_kernel, tiles_per_seq),
        grid=(n // ATTN_TILE,),
        in_specs=[pl.BlockSpec((ATTN_TILE, D_MODEL), lambda i: (i, 0)),
                  _const_spec((1, D_MODEL)),
                  _const_spec((D_MODEL, n_cols)),
                  _const_spec((2 * DIFF_WIDTH, D_MODEL)),
                  _const_spec((1, RWKV_COLS))],
        out_specs=[pl.BlockSpec((ATTN_TILE, RWKV_COLS), lambda i: (i, 0)),
                   pl.BlockSpec((ATTN_TILE, DIFF_WIDTH), lambda i: (i, 0)),
                   t_spec, t_spec],
        out_shape=[jax.ShapeDtypeStruct((n, RWKV_COLS), F32),
                   jax.ShapeDtypeStruct((n, DIFF_WIDTH), BF16),
                   t_shape, t_shape],
        scratch_shapes=[pltpu.VMEM((1, RWKV_COLS), F32)],
        compiler_params=_params("arbitrary"),
        name="in_proj",
    )(x2, norm_w, w_in_bf, w_qvt_bf, mu)


GROUP_LANES = 256
HEADS_PER_GROUP = GROUP_LANES // RWKV_HEAD
N_GROUPS = RWKV_WIDTH // GROUP_LANES
LANES = 128
RWKV_PREP_BF16 = ("kd", "rd", "bi", "ki", "bw", "kw", "v")
RWKV_PREP_F32 = ("bonus", "gate")


def _rwkv_tile_masks():
    r = jnp.arange(CHUNK)[:, None]
    c = (jnp.arange(GROUP_LANES) % CHUNK)[None, :]
    ms = [r > c, r >= c, r == c, (r > c) & ((r // 8) == (c // 8))]
    size = 8
    while size < CHUNK:
        ms.append(((r // (2 * size)) == (c // (2 * size))) & ((r // size) > (c // size)))
        size *= 2
    return jnp.stack(ms).astype(F32)


def _rwkv_kernel(x0_ref, xa_ref, xb_ref, w0_ref, wdec_ref, a0_ref, aup_ref, gup_ref, kk_ref, ka_ref,
                 rk_ref, lnw_ref, lnb_ref, seg_ref, tril_ref, tmask_ref, o_ref, st_ref, *prep_refs):
    T, W, G = CHUNK, RWKV_WIDTH, GROUP_LANES
    hb = xa_ref.shape[0]
    n_names = len(RWKV_PREP_BF16) + len(RWKV_PREP_F32) + 1
    names = RWKV_PREP_BF16 + RWKV_PREP_F32 + ("e_last",)
    stage_a = dict(zip(names, prep_refs[:n_names]))
    stage_b = dict(zip(names, prep_refs[n_names:]))

    seg = seg_ref[...]

    def head_sum(x):
        xb = x.astype(BF16)
        return jnp.concatenate(
            [jnp.dot(xb[:, g * G:(g + 1) * G], seg, preferred_element_type=F32)
             for g in range(N_GROUPS)], axis=1)

    def prepare(x_ref, stage):
        xx = jnp.concatenate([x_ref[b] for b in range(hb)], axis=0)
        r = xx[:, 0:W]
        k = xx[:, W:2 * W]
        v = xx[:, 2 * W:3 * W]
        lora_da = xx[:, 3 * W:3 * W + DECAY_LORA + AAA_LORA]
        gd = xx[:, 3 * W + DECAY_LORA + AAA_LORA:]

        z = w0_ref[...] + _mm(jnp.tanh(lora_da), wdec_ref[...])
        yield
        softplus_neg = jnp.maximum(-z, 0.0) + jnp.log(1.0 + jnp.exp(-jnp.abs(z)))
        yield
        log_decay = -jnp.exp(-softplus_neg - 0.5)
        yield
        a = jax.nn.sigmoid(a0_ref[...] + _mm(lora_da, aup_ref[...]))
        yield
        stage["gate"][...] = _mm(jax.nn.sigmoid(gd), gup_ref[...])
        yield

        kk = k * kk_ref[...]
        kk = kk * lax.rsqrt(jnp.maximum(head_sum(kk * kk), 1e-24))
        yield
        k_mod = k * (1.0 + (a - 1.0) * ka_ref[...])
        b_vec = kk * a
        yield

        ld_hi = log_decay.astype(BF16)
        ld_lo = (log_decay - ld_hi.astype(F32)).astype(BF16)
        tril = tril_ref[...]
        cl = (jnp.dot(tril, ld_hi, preferred_element_type=F32)
              + jnp.dot(tril, ld_lo, preferred_element_type=F32))
        yield
        e_incl = jnp.exp(cl)
        yield
        e_inv = jnp.exp(-cl)
        yield
        bi = b_vec * e_inv
        stage["bi"][...] = bi.astype(BF16)
        yield
        ki = k_mod * e_inv
        stage["ki"][...] = ki.astype(BF16)
        yield
        stage["kd"][...] = (kk * jnp.exp(cl - log_decay)).astype(BF16)
        yield
        stage["rd"][...] = (r * e_incl).astype(BF16)
        stage["v"][...] = v.astype(BF16)
        yield
        stage["bonus"][...] = head_sum(r * k_mod * rk_ref[...]) * v
        yield
        for b in range(hb):
            rows = slice(b * T, (b + 1) * T)
            w_total = e_incl[b * T + T - 1:b * T + T, :]
            stage["bw"][rows, :] = (bi[rows] * w_total).astype(BF16)
            stage["kw"][rows, :] = (ki[rows] * w_total).astype(BF16)
            stage["e_last"][b] = jnp.broadcast_to(w_total, (8, W))
            if b % 2 == 1:
                yield

    bdm32 = seg.astype(F32)
    m_strict, m_incl, m_eye, m_blk8 = (tmask_ref[i] for i in range(4))
    m_levels = [tmask_ref[i] for i in range(4, tmask_ref.shape[0])]
    m_both = jnp.concatenate([m_strict, m_incl], axis=0)

    lane_in_vreg = lax.broadcasted_iota(jnp.int32, (T, G), 1) % LANES
    half_masks = [jnp.where((lane_in_vreg // RWKV_HEAD) == j, 1.0, 0.0).astype(BF16)
                  for j in range(LANES // RWKV_HEAD)]
    zero_vreg_cols = jnp.zeros((T, LANES), BF16)

    def bd(y):
        yb = y.astype(BF16)
        kept = [yb * m for m in half_masks]
        rows = []
        for h in range(HEADS_PER_GROUP):
            col, j = divmod(h * RWKV_HEAD, LANES)
            j //= RWKV_HEAD
            rows.append(jnp.concatenate(
                [kept[j][:, c * LANES:(c + 1) * LANES] if c == col else zero_vreg_cols
                 for c in range(G // LANES)], axis=1))
        return jnp.concatenate(rows, axis=0)

    def recurrence(stage, b0):
        items = [(b, g) for b in range(hb) for g in range(N_GROUPS)]
        n_it = range(len(items))
        tile = lambda name, b, g: stage[name][b * T:(b + 1) * T, g * G:(g + 1) * G]
        kd = [tile("kd", b, g) for b, g in items]
        bi = [tile("bi", b, g) for b, g in items]
        ki = [tile("ki", b, g) for b, g in items]
        vv = [tile("v", b, g) for b, g in items]

        w_col = {}
        for b in range(hb):
            col = jnp.broadcast_to(stage["e_last"][b][0:1, :], (LANES, W)).T
            for g in range(N_GROUPS):
                w_col[(b, g)] = jnp.concatenate([col[g * G:(g + 1) * G]] * (G // LANES), axis=1)

        lhs = [jnp.concatenate([kd[i], tile("rd", *items[i])], axis=0) for i in n_it]
        a1 = [_mm_nt(lhs[i], bd(bi[i])) for i in n_it]
        yield
        a2 = [_mm_nt(lhs[i], bd(ki[i])) for i in n_it]
        yield
        a_k = [a2[i] * m_both for i in n_it]
        a_rb = [a1[i][T:] * m_incl for i in n_it]
        yield

        n0 = [a1[i][:T] * m_blk8 for i in n_it]
        n2 = [_mm(n0[i], bd(n0[i])) for i in n_it]
        yield
        n4 = [_mm(n2[i], bd(n2[i])) for i in n_it]
        yield
        x = [_mm(m_eye - n0[i], bd(m_eye + n2[i])) for i in n_it]
        yield
        x = [_mm(x[i], bd(m_eye + n4[i])) for i in n_it]
        yield
        for m_off in m_levels:
            t1 = [_mm(x[i], bd(a1[i][:T] * m_off)) for i in n_it]
            yield
            x = [x[i] - _mm(t1[i], bd(x[i])) for i in n_it]
            yield

        st = [st_ref[(b0 + b) * N_GROUPS + g] for b, g in items]
        ks = [jnp.dot(lhs[i], st[i].astype(BF16), preferred_element_type=F32) for i in n_it]
        yield
        sv = [ks[i] + _mm(a_k[i], bd(vv[i])) for i in n_it]
        yield
        u = [-_mm(x[i], bd(sv[i][:T])) for i in n_it]
        yield
        y_tiles = [sv[i][T:] + _mm(a_rb[i], bd(u[i])) for i in n_it]
        yield
        for i, (b, g) in enumerate(items):
            wk = jnp.concatenate([tile("bw", b, g), tile("kw", b, g)], axis=0)
            grow = _mm_tn(wk, jnp.concatenate([u[i].astype(BF16), vv[i]], axis=0)) * bdm32
            st_ref[(b0 + b) * N_GROUPS + g] = st[i] * w_col[(b, g)] + grow
            if i % 4 == 3:
                yield

        y = jnp.concatenate(
            [jnp.concatenate(y_tiles[b * N_GROUPS:(b + 1) * N_GROUPS], axis=1) for b in range(hb)],
            axis=0)
        inv_n = 1.0 / RWKV_HEAD
        mean = head_sum(y) * inv_n
        d = y - mean
        var = head_sum(d * d) * inv_n
        yield
        yn = d * lax.rsqrt(var + GN_EPS) * lnw_ref[...] + lnb_ref[...]
        out = ((yn + stage["bonus"][...]) * stage["gate"][...]).astype(o_ref.dtype)
        for b in range(hb):
            o_ref[b0 + b] = out[b * T:(b + 1) * T]

    @pl.when(pl.program_id(0) == 0)
    def _():
        st_ref[...] = jnp.zeros_like(st_ref)
        _interleave(prepare(x0_ref, stage_a))

    _interleave(recurrence(stage_a, 0), prepare(xb_ref, stage_b))
    _interleave(recurrence(stage_b, hb), prepare(xa_ref, stage_a))


def _rwkv(xx, w0, wdec_pad, a0, aup_pad, gup, k_k, k_a, r_k, lnx_w, lnx_b):
    B, S, _ = xx.shape
    W, G = RWKV_WIDTH, GROUP_LANES
    hb = B // 2
    n_chunks = S // CHUNK
    head = jnp.arange(G) // RWKV_HEAD
    seg = (head[:, None] == head[None, :]).astype(BF16)
    t = jnp.arange(hb * CHUNK)
    tril = ((t[:, None] >= t[None, :]) & (t[:, None] // CHUNK == t[None, :] // CHUNK)).astype(BF16)
    tmasks = _rwkv_tile_masks()
    vec = lambda n: _const_spec((1, n))
    half = lambda dtype: pltpu.VMEM((hb * CHUNK, W), dtype)
    stage = ([half(BF16)] * len(RWKV_PREP_BF16) + [half(F32)] * len(RWKV_PREP_F32)
             + [pltpu.VMEM((hb, 8, W), F32)])
    return pl.pallas_call(
        _rwkv_kernel,
        grid=(n_chunks,),
        in_specs=[pl.BlockSpec((hb, CHUNK, RWKV_COLS), lambda j: (0, 0, 0)),
                  pl.BlockSpec((hb, CHUNK, RWKV_COLS), lambda j: (0, jnp.minimum(j + 1, n_chunks - 1), 0)),
                  pl.BlockSpec((hb, CHUNK, RWKV_COLS), lambda j: (1, j, 0)),
                  vec(W), _const_spec((LANES, W)), vec(W), _const_spec((LANES, W)),
                  _const_spec((GATE_LORA, W)), vec(W), vec(W), vec(W), vec(W), vec(W),
                  _const_spec((G, G)), _const_spec((hb * CHUNK, hb * CHUNK)),
                  _const_spec(tmasks.shape)],
        out_specs=pl.BlockSpec((B, CHUNK, W), lambda j: (0, j, 0)),
        out_shape=jax.ShapeDtypeStruct((B, S, W), BF16),
        scratch_shapes=[pltpu.VMEM((B * N_GROUPS, G, G), F32)] + stage + stage,
        compiler_params=_params("arbitrary"),
        name="rwkv7",
    )(xx, xx, xx, w0, wdec_pad, a0, aup_pad, gup, k_k, k_a, r_k, lnx_w, lnx_b, seg, tril, tmasks)


def _diff_attn_kernel(lam_init, qt_in_ref, k_ref, vt_ref, lq1_ref, lk1_ref, lq2_ref, lk2_ref, sw_ref,
                      o_ref, qt_ref, m_ref, l_ref, acc_ref, sa_ref, sb_ref, sc_ref):
    tq = tk = ATTN_TILE
    D, H = DIFF_VDIM, DIFF_HEADS
    hs = range(H)
    i = pl.program_id(1)
    n_q = pl.num_programs(1) - 1

    def build_queries():
        d_row = lax.broadcasted_iota(jnp.int32, (D, tq), 0)
        for h in hs:
            qt = qt_in_ref[0, 0, h * D:(h + 1) * D, :]
            zero = jnp.zeros_like(qt)
            qt_ref[h] = jnp.concatenate([jnp.where(d_row < DIFF_QKDIM, qt, zero),
                                         jnp.where(d_row >= DIFF_QKDIM, qt, zero)], axis=1)

    def produce(t, s_ref):
        off = pl.multiple_of(t * tk, tk)
        for h in hs:
            s_ref[h] = jnp.dot(k_ref[0, pl.ds(off, tk), h * D:(h + 1) * D], qt_ref[h],
                               preferred_element_type=F32)
            yield

    def consume(t, s_ref, mask):
        s = [s_ref[h] for h in hs]
        if mask is not None:
            s = [jnp.where(mask, s[h], NEG_INF) for h in hs]
        m_old = [m_ref[h] for h in hs]
        m_new = [jnp.maximum(m_old[h], jnp.max(s[h], axis=0, keepdims=True)) for h in hs]
        yield
        alpha = [jnp.exp2(m_old[h] - m_new[h]) for h in hs]
        p = [jnp.exp2(s[h] - m_new[h]) for h in hs]
        yield
        pv = [jnp.dot(vt_ref[0, t, h * D:(h + 1) * D, :], p[h].astype(BF16),
                      preferred_element_type=F32) for h in hs]
        yield
        for h in hs:
            m_ref[h] = m_new[h]
            l_ref[h] = alpha[h] * l_ref[h] + jnp.sum(p[h], axis=0, keepdims=True)
            acc_ref[h] = alpha[h] * acc_ref[h] + pv[h]
        yield

    def finish_previous():
        key_chunk = lax.broadcasted_iota(jnp.int32, (tk, 2 * tq), 0) // CHUNK
        qry_chunk = (lax.broadcasted_iota(jnp.int32, (tk, 2 * tq), 1) % tq) // CHUNK
        yield from consume(i - 1, sc_ref, key_chunk <= qry_chunk)
        lam = (jnp.exp(jnp.sum(lq1_ref[...] * lk1_ref[...], axis=-1, keepdims=True))
               - jnp.exp(jnp.sum(lq2_ref[...] * lk2_ref[...], axis=-1, keepdims=True)) + lam_init)
        outs = []
        for h in hs:
            o2 = acc_ref[h] / l_ref[h]
            o = o2[:, :tq] - lam * o2[:, tq:]
            o = o * lax.rsqrt(jnp.mean(o * o, axis=0, keepdims=True) + RMS_EPS) * sw_ref[...]
            outs.append((o * (1.0 - lam_init)).T)
        o_ref[0] = jnp.concatenate(outs, axis=1).astype(o_ref.dtype)

    def reset_softmax_state():
        m_ref[...] = jnp.full(m_ref.shape, NEG_INF, F32)
        l_ref[...] = jnp.zeros(l_ref.shape, F32)
        acc_ref[...] = jnp.zeros(acc_ref.shape, F32)

    @pl.when(i == 0)
    def _():
        build_queries()
        _interleave(produce(0, sc_ref))
        reset_softmax_state()

    @pl.when(jnp.logical_and(i > 0, i < n_q))
    def _():
        build_queries()
        _interleave(produce(0, sa_ref), finish_previous())
        reset_softmax_state()

    @pl.when(i == n_q)
    def _():
        _interleave(finish_previous())

    def pair(pi, carry):
        t = 2 * pi
        _interleave(produce(t + 1, sb_ref), consume(t, sa_ref, None))
        _interleave(produce(t + 2, sa_ref), consume(t + 1, sb_ref, None))
        return carry

    n_regular = jnp.maximum(i - 1, 0)
    lax.fori_loop(0, jnp.where(i < n_q, n_regular // 2, 0), pair, 0)

    @pl.when(jnp.logical_and(jnp.logical_and(i > 0, i < n_q), n_regular % 2 == 1))
    def _():
        _interleave(produce(i - 1, sb_ref), consume(i - 2, sa_ref, None))
        _interleave(produce(i, sc_ref), consume(i - 1, sb_ref, None))

    @pl.when(jnp.logical_and(jnp.logical_and(i > 0, i < n_q), n_regular % 2 == 0))
    def _():
        _interleave(produce(i, sc_ref), consume(i - 1, sa_ref, None))


def _diff_attn(q_t, k, v_t, lam_q1, lam_k1, lam_q2, lam_k2, subln_col, lam_init):
    B, S, _ = k.shape
    H, D, tq = DIFF_HEADS, DIFF_VDIM, ATTN_TILE
    n_q = S // tq
    lamv = _const_spec((1, DIFF_QKDIM))
    scores = pltpu.VMEM((H, tq, 2 * tq), F32)
    return pl.pallas_call(
        functools.partial(_diff_attn_kernel, lam_init),
        grid=(B, n_q + 1),
        in_specs=[pl.BlockSpec((1, 1, DIFF_WIDTH, tq), lambda b, i: (b, jnp.minimum(i, n_q - 1), 0, 0)),
                  pl.BlockSpec((1, S, DIFF_WIDTH), lambda b, i: (b, 0, 0)),
                  pl.BlockSpec((1, n_q, DIFF_WIDTH, tq), lambda b, i: (b, 0, 0, 0)),
                  lamv, lamv, lamv, lamv, _const_spec((D, 1))],
        out_specs=pl.BlockSpec((1, tq, DIFF_WIDTH), lambda b, i: (b, jnp.maximum(i - 1, 0), 0)),
        out_shape=jax.ShapeDtypeStruct((B, S, DIFF_WIDTH), BF16),
        scratch_shapes=[pltpu.VMEM((H, D, 2 * tq), BF16),
                        pltpu.VMEM((H, 1, 2 * tq), F32), pltpu.VMEM((H, 1, 2 * tq), F32),
                        pltpu.VMEM((H, D, 2 * tq), F32), scores, scores, scores],
        compiler_params=_params("parallel", "arbitrary"),
        name="diff_attn",
    )(q_t, k, v_t, lam_q1, lam_k1, lam_q2, lam_k2, subln_col)


def _mem_kv_kernel(m_ref, nw_ref, wk_ref, wv_ref, k_ref, v_ref):
    mn = _rms(m_ref[...], nw_ref[...]).astype(BF16)
    k_ref[...] = jnp.dot(mn, wk_ref[...].astype(BF16), preferred_element_type=F32).astype(BF16)
    v_ref[...] = jnp.dot(mn, wv_ref[...].astype(BF16), preferred_element_type=F32).astype(BF16)


def _mem_kv(mem2, norm_w, w_mk, w_mv):
    n = mem2.shape[0]
    wspec = _const_spec((D_MODEL, D_MODEL))
    rows = pl.BlockSpec((ROW_TILE, D_MODEL), lambda i: (i, 0))
    return pl.pallas_call(
        _mem_kv_kernel,
        grid=(n // ROW_TILE,),
        in_specs=[rows, _const_spec((1, D_MODEL)), wspec, wspec],
        out_specs=[rows, rows],
        out_shape=[jax.ShapeDtypeStruct((n, D_MODEL), BF16)] * 2,
        compiler_params=_params("parallel"),
        name="mem_kv",
    )(mem2, norm_w, w_mk, w_mv)


def _mix_mem_kernel(x_ref, yr_ref, yd_ref, wo_ref, nw_ref, wq_ref, k_ref, v_ref, wmo_ref, o_ref):
    h = (x_ref[0]
         + jnp.dot(yr_ref[0], wo_ref[:RWKV_WIDTH, :].astype(BF16), preferred_element_type=F32)
         + jnp.dot(yd_ref[0], wo_ref[RWKV_WIDTH:, :].astype(BF16), preferred_element_type=F32))
    hn = _rms(h, nw_ref[...]).astype(BF16)
    q = jnp.dot(hn, wq_ref[...].astype(BF16), preferred_element_type=F32) * (MEM_HEAD_DIM ** -0.5)
    q = q.astype(BF16)
    outs = []
    for hd in range(MEM_HEADS):
        sl = slice(hd * MEM_HEAD_DIM, (hd + 1) * MEM_HEAD_DIM)
        s = lax.dot_general(q[:, sl], k_ref[0, :, sl], (((1,), (1,)), ((), ())),
                            preferred_element_type=F32)
        pr = jnp.exp(s - jnp.max(s, axis=-1, keepdims=True))
        den = jnp.sum(pr, axis=-1, keepdims=True)
        outs.append(jnp.dot(pr.astype(BF16), v_ref[0, :, sl], preferred_element_type=F32) / den)
    o = jnp.concatenate(outs, axis=1).astype(BF16)
    o_ref[0] = h + jnp.dot(o, wmo_ref[...].astype(BF16), preferred_element_type=F32)


def _mix_mem(x, y_rwkv, y_diff, w_out, norm_w, w_mq, k_mem, v_mem, w_mo):
    B, S, D = x.shape
    M = k_mem.shape[1]
    wspec = _const_spec((D, D))
    rows = lambda w: pl.BlockSpec((1, ROW_TILE, w), lambda b, i: (b, i, 0))
    memspec = pl.BlockSpec((1, M, D), lambda b, i: (b, 0, 0))
    return pl.pallas_call(
        _mix_mem_kernel,
        grid=(B, S // ROW_TILE),
        in_specs=[rows(D), rows(RWKV_WIDTH), rows(DIFF_WIDTH), wspec, _const_spec((1, D)), wspec,
                  memspec, memspec, wspec],
        out_specs=rows(D),
        out_shape=jax.ShapeDtypeStruct((B, S, D), F32),
        compiler_params=_params("parallel", "parallel"),
        name="mix_mem",
    )(x, y_rwkv, y_diff, w_out, norm_w, w_mq, k_mem, v_mem, w_mo)


def _mlp_kernel(final_norm, h_ref, nw_ref, wu_ref, wd_ref, fw_ref, o_ref):
    h = h_ref[...]
    hn = _rms(h, nw_ref[...]).astype(BF16)
    acc = h
    for c in range(D_FF // FF_TILE):
        sl = slice(c * FF_TILE, (c + 1) * FF_TILE)
        u = jnp.maximum(jnp.dot(hn, wu_ref[:, sl].astype(BF16), preferred_element_type=F32), 0.0)
        acc = acc + jnp.dot((u * u).astype(BF16), wd_ref[sl, :].astype(BF16), preferred_element_type=F32)
    o_ref[...] = _rms(acc, fw_ref[...]) if final_norm else acc


def _mlp(h2, norm_w, w_up, w_down, final_w, final_norm):
    n, D = h2.shape
    rows = pl.BlockSpec((MLP_ROW_TILE, D), lambda i: (i, 0))
    return pl.pallas_call(
        functools.partial(_mlp_kernel, final_norm),
        grid=(n // MLP_ROW_TILE,),
        in_specs=[rows, _const_spec((1, D)), _const_spec((D, D_FF)), _const_spec((D_FF, D)),
                  _const_spec((1, D))],
        out_specs=rows,
        out_shape=jax.ShapeDtypeStruct((n, D), F32),
        compiler_params=_params("parallel"),
        name="mlp",
    )(h2, norm_w, w_up, w_down, final_w)


def kernel(x, mem, norm_mix_w, w_in, mu_shift, w_decay0, w_decay_up, a0, a_up, g_up, k_k, k_a, r_k,
           lnx_w, lnx_b, lam_q1, lam_k1, lam_q2, lam_k2, subln_w, w_out, norm_mem_w, norm_src_w,
           w_mq, w_mk, w_mv, w_mo, norm_mlp_w, w_up, w_down, norm_final_w):
    B, S, D = x.shape
    depth = norm_mix_w.shape[0]
    row = lambda t: t.reshape(1, -1)
    h = x
    for l in range(depth):
        lam_init = 0.8 - 0.6 * math.exp(-0.3 * l)
        zeros_lora = jnp.zeros((DECAY_LORA, RWKV_WIDTH), BF16)
        wdec_pad = jnp.concatenate([w_decay_up[l].astype(BF16), zeros_lora], axis=0)
        aup_pad = jnp.concatenate([zeros_lora, a_up[l].astype(BF16)], axis=0)

        n_cols = RWKV_COLS + 2 * DIFF_WIDTH
        w_qv_t = jnp.concatenate([w_in[l][:, RWKV_COLS:K_COL0], w_in[l][:, n_cols:]], axis=1).T
        p_rwkv, k_att, q_t, v_t = _in_proj(h.reshape(B * S, D), row(norm_mix_w[l]),
                                           w_in[l][:, :n_cols].astype(BF16), w_qv_t.astype(BF16),
                                           row(mu_shift[l]), B)
        y_rwkv = _rwkv(p_rwkv.reshape(B, S, RWKV_COLS), row(w_decay0[l]),
                       wdec_pad, row(a0[l]), aup_pad, g_up[l].astype(BF16), row(k_k[l]),
                       row(k_a[l]), row(r_k[l]), row(lnx_w[l]), row(lnx_b[l]))
        y_diff = _diff_attn(q_t, k_att.reshape(B, S, DIFF_WIDTH), v_t, row(lam_q1[l]), row(lam_k1[l]),
                            row(lam_q2[l]), row(lam_k2[l]), subln_w[l].reshape(-1, 1), lam_init)
        k_mem, v_mem = _mem_kv(mem.reshape(-1, D), row(norm_src_w[l]), w_mk[l], w_mv[l])
        M = mem.shape[1]
        h = _mix_mem(h, y_rwkv, y_diff, w_out[l], row(norm_mem_w[l]), w_mq[l],
                     k_mem.reshape(B, M, D), v_mem.reshape(B, M, D), w_mo[l])
        h = _mlp(h.reshape(B * S, D), row(norm_mlp_w[l]), w_up[l], w_down[l], row(norm_final_w),
                 l == depth - 1)
        h = h.reshape(B, S, D)
    return h
```

```python
import functools
import math

import jax
import jax.numpy as jnp
from jax import lax
from jax.experimental import pallas as pl
from jax.experimental.pallas import tpu as pltpu

F32 = jnp.float32
BF16 = jnp.bfloat16

D_MODEL = 1024
CHUNK = 64
RWKV_HEAD = 64
RWKV_WIDTH = 512
RWKV_HEADS = 8
DECAY_LORA = 64
AAA_LORA = 64
GATE_LORA = 128
RWKV_COLS = 3 * RWKV_WIDTH + DECAY_LORA + AAA_LORA + GATE_LORA
GN_EPS = 64e-5
DIFF_WIDTH = 512
DIFF_HEADS = 4
DIFF_VDIM = 128
DIFF_QKDIM = 64
DIFF_COLS = 3 * DIFF_WIDTH
D_IN_TOTAL = RWKV_COLS + DIFF_COLS
MEM_HEADS = 4
MEM_HEAD_DIM = 256
D_FF = 4 * D_MODEL
RMS_EPS = 1e-5
NEG_INF = -1e30
LOG2_E = 1.4426950408889634

VMEM_LIMIT_BYTES = 56 * 1024 * 1024

ROW_TILE = 1024
MLP_ROW_TILE = 512
ATTN_TILE = 512
FF_TILE = 1024


def _mm(a, b):
    return jnp.dot(a.astype(BF16), b.astype(BF16), preferred_element_type=F32)


def _mm_nt(a, b):
    return lax.dot_general(a.astype(BF16), b.astype(BF16), (((1,), (1,)), ((), ())),
                           preferred_element_type=F32)


def _mm_tn(a, b):
    return lax.dot_general(a.astype(BF16), b.astype(BF16), (((0,), (0,)), ((), ())),
                           preferred_element_type=F32)


def _rms(x, w, eps=RMS_EPS):
    return x * lax.rsqrt(jnp.mean(x * x, axis=-1, keepdims=True) + eps) * w


def _params(*sem, flags=None):
    return pltpu.CompilerParams(dimension_semantics=sem, vmem_limit_bytes=VMEM_LIMIT_BYTES,
                                flags=flags)


def _interleave(*gens):
    live = list(gens)
    while live:
        for gen in list(live):
            if next(gen, StopIteration) is StopIteration:
                live.remove(gen)


def _const_spec(shape):
    nd = len(shape)
    return pl.BlockSpec(shape, lambda *_: (0,) * nd, pipeline_mode=pl.Buffered(1))


K_COL0 = RWKV_COLS + DIFF_WIDTH


def _in_proj_kernel(tiles_per_seq, x_ref, nw_ref, w_ref, wqvt_ref, mu_ref, pr_ref, k_ref, qt_ref, vt_ref,
                    carry_ref):
    i = pl.program_id(0)
    rows = x_ref.shape[0]

    @pl.when(i == 0)
    def _():
        carry_ref[...] = jnp.zeros_like(carry_ref)

    xn = _rms(x_ref[...], nw_ref[...]).astype(BF16)
    p = jnp.dot(xn, w_ref[:, :RWKV_COLS], preferred_element_type=F32)
    last_prev = jnp.where(i % tiles_per_seq == 0, 0.0, carry_ref[...])
    first_row = lax.broadcasted_iota(jnp.int32, p.shape, 0) == 0
    p_prev = jnp.where(first_row, last_prev, pltpu.roll(p, 1, 0))
    carry_ref[...] = p[rows - 1:rows, :]
    pr_ref[...] = p + (p_prev - p) * mu_ref[...]
    k_ref[...] = jnp.dot(xn, w_ref[:, K_COL0:], preferred_element_type=F32).astype(BF16)
    qv_t = lax.dot_general(wqvt_ref[...], xn, (((1,), (1,)), ((), ())), preferred_element_type=F32)
    qt_ref[0, 0] = (qv_t[:DIFF_WIDTH] * (DIFF_QKDIM ** -0.5 * LOG2_E)).astype(BF16)
    vt_ref[0, 0] = qv_t[DIFF_WIDTH:].astype(BF16)


def _in_proj(x2, norm_w, w_in_bf, w_qvt_bf, mu, batch):
    n = x2.shape[0]
    tiles_per_seq = n // batch // ATTN_TILE
    n_cols = RWKV_COLS + 2 * DIFF_WIDTH
    t_spec = pl.BlockSpec((1, 1, DIFF_WIDTH, ATTN_TILE),
                          lambda i: (i // tiles_per_seq, i % tiles_per_seq, 0, 0))
    t_shape = jax.ShapeDtypeStruct((batch, tiles_per_seq, DIFF_WIDTH, ATTN_TILE), BF16)
    return pl.pallas_call(
        functools.partial(_in_proj_kernel, tiles_per_seq),
        grid=(n // ATTN_TILE,),
        in_specs=[pl.BlockSpec((ATTN_TILE, D_MODEL), lambda i: (i, 0)),
                  _const_spec((1, D_MODEL)),
                  _const_spec((D_MODEL, n_cols)),
                  _const_spec((2 * DIFF_WIDTH, D_MODEL)),
                  _const_spec((1, RWKV_COLS))],
        out_specs=[pl.BlockSpec((ATTN_TILE, RWKV_COLS), lambda i: (i, 0)),
                   pl.BlockSpec((ATTN_TILE, DIFF_WIDTH), lambda i: (i, 0)),
                   t_spec, t_spec],
        out_shape=[jax.ShapeDtypeStruct((n, RWKV_COLS), F32),
                   jax.ShapeDtypeStruct((n, DIFF_WIDTH), BF16),
                   t_shape, t_shape],
        scratch_shapes=[pltpu.VMEM((1, RWKV_COLS), F32)],
        compiler_params=_params("arbitrary"),
        name="in_proj",
    )(x2, norm_w, w_in_bf, w_qvt_bf, mu)


GROUP_LANES = 256
HEADS_PER_GROUP = GROUP_LANES // RWKV_HEAD
N_GROUPS = RWKV_WIDTH // GROUP_LANES
LANES = 128
RWKV_PREP_BF16 = ("kd", "rd", "bi", "ki", "bw", "kw", "v")
RWKV_PREP_F32 = ("bonus", "gate")


def _rwkv_tile_masks():
    r = jnp.arange(CHUNK)[:, None]
    c = (jnp.arange(GROUP_LANES) % CHUNK)[None, :]
    ms = [r > c, r >= c, r == c, (r > c) & ((r // 8) == (c // 8))]
    size = 8
    while size < CHUNK:
        ms.append(((r // (2 * size)) == (c // (2 * size))) & ((r // size) > (c // size)))
        size *= 2
    return jnp.stack(ms).astype(F32)


def _rwkv_kernel(x0_ref, xa_ref, xb_ref, w0_ref, wdec_ref, a0_ref, aup_ref, gup_ref, kk_ref, ka_ref,
                 rk_ref, lnw_ref, lnb_ref, seg_ref, tril_ref, tmask_ref, o_ref, st_ref, *prep_refs):
    T, W, G = CHUNK, RWKV_WIDTH, GROUP_LANES
    hb = xa_ref.shape[0]
    n_names = len(RWKV_PREP_BF16) + len(RWKV_PREP_F32) + 1
    names = RWKV_PREP_BF16 + RWKV_PREP_F32 + ("e_last",)
    stage_a = dict(zip(names, prep_refs[:n_names]))
    stage_b = dict(zip(names, prep_refs[n_names:]))

    seg = seg_ref[...]

    def head_sum(x):
        xb = x.astype(BF16)
        return jnp.concatenate(
            [jnp.dot(xb[:, g * G:(g + 1) * G], seg, preferred_element_type=F32)
             for g in range(N_GROUPS)], axis=1)

    def prepare(x_ref, stage):
        xx = jnp.concatenate([x_ref[b] for b in range(hb)], axis=0)
        r = xx[:, 0:W]
        k = xx[:, W:2 * W]
        v = xx[:, 2 * W:3 * W]
        lora_da = xx[:, 3 * W:3 * W + DECAY_LORA + AAA_LORA]
        gd = xx[:, 3 * W + DECAY_LORA + AAA_LORA:]

        z = w0_ref[...] + _mm(jnp.tanh(lora_da), wdec_ref[...])
        yield
        softplus_neg = jnp.maximum(-z, 0.0) + jnp.log(1.0 + jnp.exp(-jnp.abs(z)))
        yield
        log_decay = -jnp.exp(-softplus_neg - 0.5)
        yield
        a = jax.nn.sigmoid(a0_ref[...] + _mm(lora_da, aup_ref[...]))
        yield
        stage["gate"][...] = _mm(jax.nn.sigmoid(gd), gup_ref[...])
        yield

        kk = k * kk_ref[...]
        kk = kk * lax.rsqrt(jnp.maximum(head_sum(kk * kk), 1e-24))
        yield
        k_mod = k * (1.0 + (a - 1.0) * ka_ref[...])
        b_vec = kk * a
        yield

        ld_hi = log_decay.astype(BF16)
        ld_lo = (log_decay - ld_hi.astype(F32)).astype(BF16)
        tril = tril_ref[...]
        cl = (jnp.dot(tril, ld_hi, preferred_element_type=F32)
              + jnp.dot(tril, ld_lo, preferred_element_type=F32))
        yield
        e_incl = jnp.exp(cl)
        yield
        e_inv = jnp.exp(-cl)
        yield
        bi = b_vec * e_inv
        stage["bi"][...] = bi.astype(BF16)
        yield
        ki = k_mod * e_inv
        stage["ki"][...] = ki.astype(BF16)
        yield
        stage["kd"][...] = (kk * jnp.exp(cl - log_decay)).astype(BF16)
        yield
        stage["rd"][...] = (r * e_incl).astype(BF16)
        stage["v"][...] = v.astype(BF16)
        yield
        stage["bonus"][...] = head_sum(r * k_mod * rk_ref[...]) * v
        yield
        for b in range(hb):
            rows = slice(b * T, (b + 1) * T)
            w_total = e_incl[b * T + T - 1:b * T + T, :]
            stage["bw"][rows, :] = (bi[rows] * w_total).astype(BF16)
            stage["kw"][rows, :] = (ki[rows] * w_total).astype(BF16)
            stage["e_last"][b] = jnp.broadcast_to(w_total, (8, W))
            if b % 2 == 1:
                yield

    bdm32 = seg.astype(F32)
    m_strict, m_incl, m_eye, m_blk8 = (tmask_ref[i] for i in range(4))
    m_levels = [tmask_ref[i] for i in range(4, tmask_ref.shape[0])]
    m_both = jnp.concatenate([m_strict, m_incl], axis=0)

    lane_in_vreg = lax.broadcasted_iota(jnp.int32, (T, G), 1) % LANES
    half_masks = [jnp.where((lane_in_vreg // RWKV_HEAD) == j, 1.0, 0.0).astype(BF16)
                  for j in range(LANES // RWKV_HEAD)]
    zero_vreg_cols = jnp.zeros((T, LANES), BF16)

    def bd(y):
        yb = y.astype(BF16)
        kept = [yb * m for m in half_masks]
        rows = []
        for h in range(HEADS_PER_GROUP):
            col, j = divmod(h * RWKV_HEAD, LANES)
            j //= RWKV_HEAD
            rows.append(jnp.concatenate(
                [kept[j][:, c * LANES:(c + 1) * LANES] if c == col else zero_vreg_cols
                 for c in range(G // LANES)], axis=1))
        return jnp.concatenate(rows, axis=0)

    def recurrence(stage, b0):
        items = [(b, g) for b in range(hb) for g in range(N_GROUPS)]
        n_it = range(len(items))
        tile = lambda name, b, g: stage[name][b * T:(b + 1) * T, g * G:(g + 1) * G]
        kd = [tile("kd", b, g) for b, g in items]
        bi = [tile("bi", b, g) for b, g in items]
        ki = [tile("ki", b, g) for b, g in items]
        vv = [tile("v", b, g) for b, g in items]

        w_col = {}
        for b in range(hb):
            col = jnp.broadcast_to(stage["e_last"][b][0:1, :], (LANES, W)).T
            for g in range(N_GROUPS):
                w_col[(b, g)] = jnp.concatenate([col[g * G:(g + 1) * G]] * (G // LANES), axis=1)

        lhs = [jnp.concatenate([kd[i], tile("rd", *items[i])], axis=0) for i in n_it]
        a1 = [_mm_nt(lhs[i], bd(bi[i])) for i in n_it]
        yield
        a2 = [_mm_nt(lhs[i], bd(ki[i])) for i in n_it]
        yield
        a_k = [a2[i] * m_both for i in n_it]
        a_rb = [a1[i][T:] * m_incl for i in n_it]
        yield

        n0 = [a1[i][:T] * m_blk8 for i in n_it]
        n2 = [_mm(n0[i], bd(n0[i])) for i in n_it]
        yield
        n4 = [_mm(n2[i], bd(n2[i])) for i in n_it]
        yield
        x = [_mm(m_eye - n0[i], bd(m_eye + n2[i])) for i in n_it]
        yield
        x = [_mm(x[i], bd(m_eye + n4[i])) for i in n_it]
        yield
        for m_off in m_levels:
            t1 = [_mm(x[i], bd(a1[i][:T] * m_off)) for i in n_it]
            yield
            x = [x[i] - _mm(t1[i], bd(x[i])) for i in n_it]
            yield

        st = [st_ref[(b0 + b) * N_GROUPS + g] for b, g in items]
        ks = [jnp.dot(lhs[i], st[i].astype(BF16), preferred_element_type=F32) for i in n_it]
        yield
        sv = [ks[i] + _mm(a_k[i], bd(vv[i])) for i in n_it]
        yield
        u = [-_mm(x[i], bd(sv[i][:T])) for i in n_it]
        yield
        y_tiles = [sv[i][T:] + _mm(a_rb[i], bd(u[i])) for i in n_it]
        yield
        for i, (b, g) in enumerate(items):
            wk = jnp.concatenate([tile("bw", b, g), tile("kw", b, g)], axis=0)
            grow = _mm_tn(wk, jnp.concatenate([u[i].astype(BF16), vv[i]], axis=0)) * bdm32
            st_ref[(b0 + b) * N_GROUPS + g] = st[i] * w_col[(b, g)] + grow
            if i % 4 == 3:
                yield

        y = jnp.concatenate(
            [jnp.concatenate(y_tiles[b * N_GROUPS:(b + 1) * N_GROUPS], axis=1) for b in range(hb)],
            axis=0)
        inv_n = 1.0 / RWKV_HEAD
        mean = head_sum(y) * inv_n
        d = y - mean
        var = head_sum(d * d) * inv_n
        yield
        yn = d * lax.rsqrt(var + GN_EPS) * lnw_ref[...] + lnb_ref[...]
        out = ((yn + stage["bonus"][...]) * stage["gate"][...]).astype(o_ref.dtype)
        for b in range(hb):
            o_ref[b0 + b] = out[b * T:(b + 1) * T]

    @pl.when(pl.program_id(0) == 0)
    def _():
        st_ref[...] = jnp.zeros_like(st_ref)
        _interleave(prepare(x0_ref, stage_a))

    _interleave(recurrence(stage_a, 0), prepare(xb_ref, stage_b))
    _interleave(recurrence(stage_b, hb), prepare(xa_ref, stage_a))


def _rwkv(xx, w0, wdec_pad, a0, aup_pad, gup, k_k, k_a, r_k, lnx_w, lnx_b):
    B, S, _ = xx.shape
    W, G = RWKV_WIDTH, GROUP_LANES
    hb = B // 2
    n_chunks = S // CHUNK
    head = jnp.arange(G) // RWKV_HEAD
    seg = (head[:, None] == head[None, :]).astype(BF16)
    t = jnp.arange(hb * CHUNK)
    tril = ((t[:, None] >= t[None, :]) & (t[:, None] // CHUNK == t[None, :] // CHUNK)).astype(BF16)
    tmasks = _rwkv_tile_masks()
    vec = lambda n: _const_spec((1, n))
    half = lambda dtype: pltpu.VMEM((hb * CHUNK, W), dtype)
    stage = ([half(BF16)] * len(RWKV_PREP_BF16) + [half(F32)] * len(RWKV_PREP_F32)
             + [pltpu.VMEM((hb, 8, W), F32)])
    return pl.pallas_call(
        _rwkv_kernel,
        grid=(n_chunks,),
        in_specs=[pl.BlockSpec((hb, CHUNK, RWKV_COLS), lambda j: (0, 0, 0)),
                  pl.BlockSpec((hb, CHUNK, RWKV_COLS), lambda j: (0, jnp.minimum(j + 1, n_chunks - 1), 0)),
                  pl.BlockSpec((hb, CHUNK, RWKV_COLS), lambda j: (1, j, 0)),
                  vec(W), _const_spec((LANES, W)), vec(W), _const_spec((LANES, W)),
                  _const_spec((GATE_LORA, W)), vec(W), vec(W), vec(W), vec(W), vec(W),
                  _const_spec((G, G)), _const_spec((hb * CHUNK, hb * CHUNK)),
                  _const_spec(tmasks.shape)],
        out_specs=pl.BlockSpec((B, CHUNK, W), lambda j: (0, j, 0)),
        out_shape=jax.ShapeDtypeStruct((B, S, W), BF16),
        scratch_shapes=[pltpu.VMEM((B * N_GROUPS, G, G), F32)] + stage + stage,
        compiler_params=_params("arbitrary"),
        name="rwkv7",
    )(xx, xx, xx, w0, wdec_pad, a0, aup_pad, gup, k_k, k_a, r_k, lnx_w, lnx_b, seg, tril, tmasks)


def _diff_attn_kernel(lam_init, qt_in_ref, k_ref, vt_ref, lq1_ref, lk1_ref, lq2_ref, lk2_ref, sw_ref,
                      o_ref, qt_ref, m_ref, l_ref, acc_ref, sa_ref, sb_ref):
    tq = tk = ATTN_TILE
    D, H = DIFF_VDIM, DIFF_HEADS
    hs = range(H)
    i = pl.program_id(1)

    d_row = lax.broadcasted_iota(jnp.int32, (D, tq), 0)
    for h in hs:
        qt = qt_in_ref[0, 0, h * D:(h + 1) * D, :]
        zero = jnp.zeros_like(qt)
        qt_ref[h] = jnp.concatenate([jnp.where(d_row < DIFF_QKDIM, qt, zero),
                                     jnp.where(d_row >= DIFF_QKDIM, qt, zero)], axis=1)

    m_ref[...] = jnp.full(m_ref.shape, NEG_INF, F32)
    l_ref[...] = jnp.zeros(l_ref.shape, F32)
    acc_ref[...] = jnp.zeros(acc_ref.shape, F32)

    def produce(t, s_ref):
        off = pl.multiple_of(t * tk, tk)
        for h in hs:
            s_ref[h] = jnp.dot(k_ref[0, pl.ds(off, tk), h * D:(h + 1) * D], qt_ref[h],
                               preferred_element_type=F32)

    def consume(t, s_ref, mask):
        s = [s_ref[h] for h in hs]
        if mask is not None:
            s = [jnp.where(mask, s[h], NEG_INF) for h in hs]
        m_old = [m_ref[h] for h in hs]
        m_new = [jnp.maximum(m_old[h], jnp.max(s[h], axis=0, keepdims=True)) for h in hs]
        alpha = [jnp.exp2(m_old[h] - m_new[h]) for h in hs]
        p = [jnp.exp2(s[h] - m_new[h]) for h in hs]
        pv = [jnp.dot(vt_ref[0, t, h * D:(h + 1) * D, :], p[h].astype(BF16),
                      preferred_element_type=F32) for h in hs]
        for h in hs:
            m_ref[h] = m_new[h]
            l_ref[h] = alpha[h] * l_ref[h] + jnp.sum(p[h], axis=0, keepdims=True)
            acc_ref[h] = alpha[h] * acc_ref[h] + pv[h]

    def finish(s_ref):
        key_chunk = lax.broadcasted_iota(jnp.int32, (tk, 2 * tq), 0) // CHUNK
        qry_chunk = (lax.broadcasted_iota(jnp.int32, (tk, 2 * tq), 1) % tq) // CHUNK
        consume(i, s_ref, key_chunk <= qry_chunk)
        lam = (jnp.exp(jnp.sum(lq1_ref[...] * lk1_ref[...], axis=-1, keepdims=True))
               - jnp.exp(jnp.sum(lq2_ref[...] * lk2_ref[...], axis=-1, keepdims=True)) + lam_init)
        outs = []
        for h in hs:
            o2 = acc_ref[h] / l_ref[h]
            o = o2[:, :tq] - lam * o2[:, tq:]
            o = o * lax.rsqrt(jnp.mean(o * o, axis=0, keepdims=True) + RMS_EPS) * sw_ref[...]
            outs.append((o * (1.0 - lam_init)).T)
        o_ref[0] = jnp.concatenate(outs, axis=1).astype(o_ref.dtype)

    produce(0, sa_ref)

    def pair(pi, carry):
        t = 2 * pi
        produce(t + 1, sb_ref)
        consume(t, sa_ref, None)
        produce(t + 2, sa_ref)
        consume(t + 1, sb_ref, None)
        return carry

    lax.fori_loop(0, i // 2, pair, 0)

    @pl.when(i % 2 == 1)
    def _():
        produce(i, sb_ref)
        consume(i - 1, sa_ref, None)
        finish(sb_ref)

    @pl.when(i % 2 == 0)
    def _():
        finish(sa_ref)


def _diff_attn(q_t, k, v_t, lam_q1, lam_k1, lam_q2, lam_k2, subln_col, lam_init):
    B, S, _ = k.shape
    H, D, tq = DIFF_HEADS, DIFF_VDIM, ATTN_TILE
    lamv = _const_spec((1, DIFF_QKDIM))
    scores = pltpu.VMEM((H, tq, 2 * tq), F32)
    return pl.pallas_call(
        functools.partial(_diff_attn_kernel, lam_init),
        grid=(B, S // tq),
        in_specs=[pl.BlockSpec((1, 1, DIFF_WIDTH, tq), lambda b, i: (b, i, 0, 0)),
                  pl.BlockSpec((1, S, DIFF_WIDTH), lambda b, i: (b, 0, 0)),
                  pl.BlockSpec((1, S // tq, DIFF_WIDTH, tq), lambda b, i: (b, 0, 0, 0)),
                  lamv, lamv, lamv, lamv, _const_spec((D, 1))],
        out_specs=pl.BlockSpec((1, tq, DIFF_WIDTH), lambda b, i: (b, i, 0)),
        out_shape=jax.ShapeDtypeStruct((B, S, DIFF_WIDTH), BF16),
        scratch_shapes=[pltpu.VMEM((H, D, 2 * tq), BF16),
                        pltpu.VMEM((H, 1, 2 * tq), F32), pltpu.VMEM((H, 1, 2 * tq), F32),
                        pltpu.VMEM((H, D, 2 * tq), F32), scores, scores],
        compiler_params=_params("parallel", "arbitrary"),
        name="diff_attn",
    )(q_t, k, v_t, lam_q1, lam_k1, lam_q2, lam_k2, subln_col)


def _mem_kv_kernel(m_ref, nw_ref, wk_ref, wv_ref, k_ref, v_ref):
    mn = _rms(m_ref[...], nw_ref[...]).astype(BF16)
    k_ref[...] = jnp.dot(mn, wk_ref[...].astype(BF16), preferred_element_type=F32).astype(BF16)
    v_ref[...] = jnp.dot(mn, wv_ref[...].astype(BF16), preferred_element_type=F32).astype(BF16)


def _mem_kv(mem2, norm_w, w_mk, w_mv):
    n = mem2.shape[0]
    wspec = _const_spec((D_MODEL, D_MODEL))
    rows = pl.BlockSpec((ROW_TILE, D_MODEL), lambda i: (i, 0))
    return pl.pallas_call(
        _mem_kv_kernel,
        grid=(n // ROW_TILE,),
        in_specs=[rows, _const_spec((1, D_MODEL)), wspec, wspec],
        out_specs=[rows, rows],
        out_shape=[jax.ShapeDtypeStruct((n, D_MODEL), BF16)] * 2,
        compiler_params=_params("parallel"),
        name="mem_kv",
    )(mem2, norm_w, w_mk, w_mv)


def _mix_mem_kernel(x_ref, yr_ref, yd_ref, wo_ref, nw_ref, wq_ref, k_ref, v_ref, wmo_ref, o_ref):
    h = (x_ref[0]
         + jnp.dot(yr_ref[0], wo_ref[:RWKV_WIDTH, :].astype(BF16), preferred_element_type=F32)
         + jnp.dot(yd_ref[0], wo_ref[RWKV_WIDTH:, :].astype(BF16), preferred_element_type=F32))
    hn = _rms(h, nw_ref[...]).astype(BF16)
    q = jnp.dot(hn, wq_ref[...].astype(BF16), preferred_element_type=F32) * (MEM_HEAD_DIM ** -0.5)
    q = q.astype(BF16)
    outs = []
    for hd in range(MEM_HEADS):
        sl = slice(hd * MEM_HEAD_DIM, (hd + 1) * MEM_HEAD_DIM)
        s = lax.dot_general(q[:, sl], k_ref[0, :, sl], (((1,), (1,)), ((), ())),
                            preferred_element_type=F32)
        pr = jnp.exp(s - jnp.max(s, axis=-1, keepdims=True))
        den = jnp.sum(pr, axis=-1, keepdims=True)
        outs.append(jnp.dot(pr.astype(BF16), v_ref[0, :, sl], preferred_element_type=F32) / den)
    o = jnp.concatenate(outs, axis=1).astype(BF16)
    o_ref[0] = h + jnp.dot(o, wmo_ref[...].astype(BF16), preferred_element_type=F32)


def _mix_mem(x, y_rwkv, y_diff, w_out, norm_w, w_mq, k_mem, v_mem, w_mo):
    B, S, D = x.shape
    M = k_mem.shape[1]
    wspec = _const_spec((D, D))
    rows = lambda w: pl.BlockSpec((1, ROW_TILE, w), lambda b, i: (b, i, 0))
    memspec = pl.BlockSpec((1, M, D), lambda b, i: (b, 0, 0))
    return pl.pallas_call(
        _mix_mem_kernel,
        grid=(B, S // ROW_TILE),
        in_specs=[rows(D), rows(RWKV_WIDTH), rows(DIFF_WIDTH), wspec, _const_spec((1, D)), wspec,
                  memspec, memspec, wspec],
        out_specs=rows(D),
        out_shape=jax.ShapeDtypeStruct((B, S, D), F32),
        compiler_params=_params("parallel", "parallel"),
        name="mix_mem",
    )(x, y_rwkv, y_diff, w_out, norm_w, w_mq, k_mem, v_mem, w_mo)


def _mlp_kernel(final_norm, h_ref, nw_ref, wu_ref, wd_ref, fw_ref, o_ref):
    h = h_ref[...]
    hn = _rms(h, nw_ref[...]).astype(BF16)
    acc = h
    for c in range(D_FF // FF_TILE):
        sl = slice(c * FF_TILE, (c + 1) * FF_TILE)
        u = jnp.maximum(jnp.dot(hn, wu_ref[:, sl].astype(BF16), preferred_element_type=F32), 0.0)
        acc = acc + jnp.dot((u * u).astype(BF16), wd_ref[sl, :].astype(BF16), preferred_element_type=F32)
    o_ref[...] = _rms(acc, fw_ref[...]) if final_norm else acc


def _mlp(h2, norm_w, w_up, w_down, final_w, final_norm):
    n, D = h2.shape
    rows = pl.BlockSpec((MLP_ROW_TILE, D), lambda i: (i, 0))
    return pl.pallas_call(
        functools.partial(_mlp_kernel, final_norm),
        grid=(n // MLP_ROW_TILE,),
        in_specs=[rows, _const_spec((1, D)), _const_spec((D, D_FF)), _const_spec((D_FF, D)),
                  _const_spec((1, D))],
        out_specs=rows,
        out_shape=jax.ShapeDtypeStruct((n, D), F32),
        compiler_params=_params("parallel"),
        name="mlp",
    )(h2, norm_w, w_up, w_down, final_w)


def kernel(x, mem, norm_mix_w, w_in, mu_shift, w_decay0, w_decay_up, a0, a_up, g_up, k_k, k_a, r_k,
           lnx_w, lnx_b, lam_q1, lam_k1, lam_q2, lam_k2, subln_w, w_out, norm_mem_w, norm_src_w,
           w_mq, w_mk, w_mv, w_mo, norm_mlp_w, w_up, w_down, norm_final_w):
    B, S, D = x.shape
    depth = norm_mix_w.shape[0]
    row = lambda t: t.reshape(1, -1)
    h = x
    for l in range(depth):
        lam_init = 0.8 - 0.6 * math.exp(-0.3 * l)
        zeros_lora = jnp.zeros((DECAY_LORA, RWKV_WIDTH), BF16)
        wdec_pad = jnp.concatenate([w_decay_up[l].astype(BF16), zeros_lora], axis=0)
        aup_pad = jnp.concatenate([zeros_lora, a_up[l].astype(BF16)], axis=0)

        n_cols = RWKV_COLS + 2 * DIFF_WIDTH
        w_qv_t = jnp.concatenate([w_in[l][:, RWKV_COLS:K_COL0], w_in[l][:, n_cols:]], axis=1).T
        p_rwkv, k_att, q_t, v_t = _in_proj(h.reshape(B * S, D), row(norm_mix_w[l]),
                                           w_in[l][:, :n_cols].astype(BF16), w_qv_t.astype(BF16),
                                           row(mu_shift[l]), B)
        y_rwkv = _rwkv(p_rwkv.reshape(B, S, RWKV_COLS), row(w_decay0[l]),
                       wdec_pad, row(a0[l]), aup_pad, g_up[l].astype(BF16), row(k_k[l]),
                       row(k_a[l]), row(r_k[l]), row(lnx_w[l]), row(lnx_b[l]))
        y_diff = _diff_attn(q_t, k_att.reshape(B, S, DIFF_WIDTH), v_t, row(lam_q1[l]), row(lam_k1[l]),
                            row(lam_q2[l]), row(lam_k2[l]), subln_w[l].reshape(-1, 1), lam_init)
        k_mem, v_mem = _mem_kv(mem.reshape(-1, D), row(norm_src_w[l]), w_mk[l], w_mv[l])
        M = mem.shape[1]
        h = _mix_mem(h, y_rwkv, y_diff, w_out[l], row(norm_mem_w[l]), w_mq[l],
                     k_mem.reshape(B, M, D), v_mem.reshape(B, M, D), w_mo[l])
        h = _mlp(h.reshape(B * S, D), row(norm_mlp_w[l]), w_up[l], w_down[l], row(norm_final_w),
                 l == depth - 1)
        h = h.reshape(B, S, D)
    return h
```

```python
import functools
import math

import jax
import jax.numpy as jnp
from jax import lax
from jax.experimental import pallas as pl
from jax.experimental.pallas import tpu as pltpu

F32 = jnp.float32
BF16 = jnp.bfloat16

D_MODEL = 1024
CHUNK = 64
RWKV_HEAD = 64
RWKV_WIDTH = 512
RWKV_HEADS = 8
DECAY_LORA = 64
AAA_LORA = 64
GATE_LORA = 128
RWKV_COLS = 3 * RWKV_WIDTH + DECAY_LORA + AAA_LORA + GATE_LORA
GN_EPS = 64e-5
DIFF_WIDTH = 512
DIFF_HEADS = 4
DIFF_VDIM = 128
DIFF_QKDIM = 64
DIFF_COLS = 3 * DIFF_WIDTH
D_IN_TOTAL = RWKV_COLS + DIFF_COLS
MEM_HEADS = 4
MEM_HEAD_DIM = 256
D_FF = 4 * D_MODEL
RMS_EPS = 1e-5
NEG_INF = -1e30
LOG2_E = 1.4426950408889634

VMEM_LIMIT_BYTES = 56 * 1024 * 1024

ROW_TILE = 1024
MLP_ROW_TILE = 512
ATTN_TILE = 512
FF_TILE = 1024


def _mm(a, b):
    return jnp.dot(a.astype(BF16), b.astype(BF16), preferred_element_type=F32)


def _mm_nt(a, b):
    return lax.dot_general(a.astype(BF16), b.astype(BF16), (((1,), (1,)), ((), ())),
                           preferred_element_type=F32)


def _mm_tn(a, b):
    return lax.dot_general(a.astype(BF16), b.astype(BF16), (((0,), (0,)), ((), ())),
                           preferred_element_type=F32)


def _rms(x, w, eps=RMS_EPS):
    return x * lax.rsqrt(jnp.mean(x * x, axis=-1, keepdims=True) + eps) * w


def _params(*sem, flags=None):
    return pltpu.CompilerParams(dimension_semantics=sem, vmem_limit_bytes=VMEM_LIMIT_BYTES,
                                flags=flags)


def _interleave(*gens):
    live = list(gens)
    while live:
        for gen in list(live):
            if next(gen, StopIteration) is StopIteration:
                live.remove(gen)


def _const_spec(shape):
    nd = len(shape)
    return pl.BlockSpec(shape, lambda *_: (0,) * nd, pipeline_mode=pl.Buffered(1))


K_COL0 = RWKV_COLS + DIFF_WIDTH


def _in_proj_kernel(tiles_per_seq, x_ref, nw_ref, w_ref, wqvt_ref, mu_ref, pr_ref, k_ref, qt_ref, vt_ref,
                    carry_ref):
    i = pl.program_id(0)
    rows = x_ref.shape[0]

    @pl.when(i == 0)
    def _():
        carry_ref[...] = jnp.zeros_like(carry_ref)

    xn = _rms(x_ref[...], nw_ref[...]).astype(BF16)
    p = jnp.dot(xn, w_ref[:, :RWKV_COLS], preferred_element_type=F32)
    last_prev = jnp.where(i % tiles_per_seq == 0, 0.0, carry_ref[...])
    first_row = lax.broadcasted_iota(jnp.int32, p.shape, 0) == 0
    p_prev = jnp.where(first_row, last_prev, pltpu.roll(p, 1, 0))
    carry_ref[...] = p[rows - 1:rows, :]
    pr_ref[...] = p + (p_prev - p) * mu_ref[...]
    k_ref[...] = jnp.dot(xn, w_ref[:, K_COL0:], preferred_element_type=F32).astype(BF16)
    qv_t = lax.dot_general(wqvt_ref[...], xn, (((1,), (1,)), ((), ())), preferred_element_type=F32)
    qt_ref[0, 0] = (qv_t[:DIFF_WIDTH] * (DIFF_QKDIM ** -0.5 * LOG2_E)).astype(BF16)
    vt_ref[0, 0] = qv_t[DIFF_WIDTH:].astype(BF16)


def _in_proj(x2, norm_w, w_in_bf, w_qvt_bf, mu, batch):
    n = x2.shape[0]
    tiles_per_seq = n // batch // ATTN_TILE
    n_cols = RWKV_COLS + 2 * DIFF_WIDTH
    t_spec = pl.BlockSpec((1, 1, DIFF_WIDTH, ATTN_TILE),
                          lambda i: (i // tiles_per_seq, i % tiles_per_seq, 0, 0))
    t_shape = jax.ShapeDtypeStruct((batch, tiles_per_seq, DIFF_WIDTH, ATTN_TILE), BF16)
    return pl.pallas_call(
        functools.partial(_in_proj_kernel, tiles_per_seq),
        grid=(n // ATTN_TILE,),
        in_specs=[pl.BlockSpec((ATTN_TILE, D_MODEL), lambda i: (i, 0)),
                  _const_spec((1, D_MODEL)),
                  _const_spec((D_MODEL, n_cols)),
                  _const_spec((2 * DIFF_WIDTH, D_MODEL)),
                  _const_spec((1, RWKV_COLS))],
        out_specs=[pl.BlockSpec((ATTN_TILE, RWKV_COLS), lambda i: (i, 0)),
                   pl.BlockSpec((ATTN_TILE, DIFF_WIDTH), lambda i: (i, 0)),
                   t_spec, t_spec],
        out_shape=[jax.ShapeDtypeStruct((n, RWKV_COLS), F32),
                   jax.ShapeDtypeStruct((n, DIFF_WIDTH), BF16),
                   t_shape, t_shape],
        scratch_shapes=[pltpu.VMEM((1, RWKV_COLS), F32)],
        compiler_params=_params("arbitrary"),
        name="in_proj",
    )(x2, norm_w, w_in_bf, w_qvt_bf, mu)


GROUP_LANES = 256
HEADS_PER_GROUP = GROUP_LANES // RWKV_HEAD
N_GROUPS = RWKV_WIDTH // GROUP_LANES
LANES = 128
RWKV_PREP_BF16 = ("kd", "rd", "bi", "ki", "bw", "kw", "v")
RWKV_PREP_F32 = ("bonus", "gate")


def _rwkv_tile_masks():
    r = jnp.arange(CHUNK)[:, None]
    c = (jnp.arange(GROUP_LANES) % CHUNK)[None, :]
    ms = [r > c, r >= c, r == c, (r > c) & ((r // 8) == (c // 8))]
    size = 8
    while size < CHUNK:
        ms.append(((r // (2 * size)) == (c // (2 * size))) & ((r // size) > (c // size)))
        size *= 2
    return jnp.stack(ms).astype(F32)


def _rwkv_kernel(x0_ref, xa_ref, xb_ref, w0_ref, wdec_ref, a0_ref, aup_ref, gup_ref, kk_ref, ka_ref,
                 rk_ref, lnw_ref, lnb_ref, seg_ref, tril_ref, tmask_ref, o_ref, st_ref, *prep_refs):
    T, W, G = CHUNK, RWKV_WIDTH, GROUP_LANES
    hb = xa_ref.shape[0]
    n_names = len(RWKV_PREP_BF16) + len(RWKV_PREP_F32) + 1
    names = RWKV_PREP_BF16 + RWKV_PREP_F32 + ("e_last",)
    stage_a = dict(zip(names, prep_refs[:n_names]))
    stage_b = dict(zip(names, prep_refs[n_names:]))

    seg = seg_ref[...]

    def head_sum(x):
        xb = x.astype(BF16)
        return jnp.concatenate(
            [jnp.dot(xb[:, g * G:(g + 1) * G], seg, preferred_element_type=F32)
             for g in range(N_GROUPS)], axis=1)

    def prepare(x_ref, stage):
        xx = jnp.concatenate([x_ref[b] for b in range(hb)], axis=0)
        r = xx[:, 0:W]
        k = xx[:, W:2 * W]
        v = xx[:, 2 * W:3 * W]
        lora_da = xx[:, 3 * W:3 * W + DECAY_LORA + AAA_LORA]
        gd = xx[:, 3 * W + DECAY_LORA + AAA_LORA:]

        z = w0_ref[...] + _mm(jnp.tanh(lora_da), wdec_ref[...])
        yield
        softplus_neg = jnp.maximum(-z, 0.0) + jnp.log(1.0 + jnp.exp(-jnp.abs(z)))
        yield
        log_decay = -jnp.exp(-softplus_neg - 0.5)
        yield
        a = jax.nn.sigmoid(a0_ref[...] + _mm(lora_da, aup_ref[...]))
        yield
        stage["gate"][...] = _mm(jax.nn.sigmoid(gd), gup_ref[...])
        yield

        kk = k * kk_ref[...]
        kk = kk * lax.rsqrt(jnp.maximum(head_sum(kk * kk), 1e-24))
        yield
        k_mod = k * (1.0 + (a - 1.0) * ka_ref[...])
        b_vec = kk * a
        yield

        ld_hi = log_decay.astype(BF16)
        ld_lo = (log_decay - ld_hi.astype(F32)).astype(BF16)
        tril = tril_ref[...]
        cl = (jnp.dot(tril, ld_hi, preferred_element_type=F32)
              + jnp.dot(tril, ld_lo, preferred_element_type=F32))
        yield
        e_incl = jnp.exp(cl)
        yield
        e_inv = jnp.exp(-cl)
        yield
        bi = b_vec * e_inv
        stage["bi"][...] = bi.astype(BF16)
        yield
        ki = k_mod * e_inv
        stage["ki"][...] = ki.astype(BF16)
        yield
        stage["kd"][...] = (kk * jnp.exp(cl - log_decay)).astype(BF16)
        yield
        stage["rd"][...] = (r * e_incl).astype(BF16)
        stage["v"][...] = v.astype(BF16)
        yield
        stage["bonus"][...] = head_sum(r * k_mod * rk_ref[...]) * v
        yield
        for b in range(hb):
            rows = slice(b * T, (b + 1) * T)
            w_total = e_incl[b * T + T - 1:b * T + T, :]
            stage["bw"][rows, :] = (bi[rows] * w_total).astype(BF16)
            stage["kw"][rows, :] = (ki[rows] * w_total).astype(BF16)
            stage["e_last"][b] = jnp.broadcast_to(w_total, (8, W))
            if b % 2 == 1:
                yield

    bdm32 = seg.astype(F32)
    m_strict, m_incl, m_eye, m_blk8 = (tmask_ref[i] for i in range(4))
    m_levels = [tmask_ref[i] for i in range(4, tmask_ref.shape[0])]
    m_both = jnp.concatenate([m_strict, m_incl], axis=0)

    lane_in_vreg = lax.broadcasted_iota(jnp.int32, (T, G), 1) % LANES
    half_masks = [jnp.where((lane_in_vreg // RWKV_HEAD) == j, 1.0, 0.0).astype(BF16)
                  for j in range(LANES // RWKV_HEAD)]
    zero_vreg_cols = jnp.zeros((T, LANES), BF16)

    def bd(y):
        yb = y.astype(BF16)
        kept = [yb * m for m in half_masks]
        rows = []
        for h in range(HEADS_PER_GROUP):
            col, j = divmod(h * RWKV_HEAD, LANES)
            j //= RWKV_HEAD
            rows.append(jnp.concatenate(
                [kept[j][:, c * LANES:(c + 1) * LANES] if c == col else zero_vreg_cols
                 for c in range(G // LANES)], axis=1))
        return jnp.concatenate(rows, axis=0)

    def recurrence(stage, b0):
        items = [(b, g) for b in range(hb) for g in range(N_GROUPS)]
        n_it = range(len(items))
        tile = lambda name, b, g: stage[name][b * T:(b + 1) * T, g * G:(g + 1) * G]
        kd = [tile("kd", b, g) for b, g in items]
        bi = [tile("bi", b, g) for b, g in items]
        ki = [tile("ki", b, g) for b, g in items]
        vv = [tile("v", b, g) for b, g in items]

        w_col = {}
        for b in range(hb):
            col = jnp.broadcast_to(stage["e_last"][b][0:1, :], (LANES, W)).T
            for g in range(N_GROUPS):
                w_col[(b, g)] = jnp.concatenate([col[g * G:(g + 1) * G]] * (G // LANES), axis=1)

        lhs = [jnp.concatenate([kd[i], tile("rd", *items[i])], axis=0) for i in n_it]
        a1 = [_mm(lhs[i], bd(bi[i]).astype(F32).T) for i in n_it]
        yield
        a2 = [_mm(lhs[i], bd(ki[i]).astype(F32).T) for i in n_it]
        yield
        a_k = [a2[i] * m_both for i in n_it]
        a_rb = [a1[i][T:] * m_incl for i in n_it]
        yield

        n0 = [a1[i][:T] * m_blk8 for i in n_it]
        n2 = [_mm(n0[i], bd(n0[i])) for i in n_it]
        yield
        n4 = [_mm(n2[i], bd(n2[i])) for i in n_it]
        yield
        x = [_mm(m_eye - n0[i], bd(m_eye + n2[i])) for i in n_it]
        yield
        x = [_mm(x[i], bd(m_eye + n4[i])) for i in n_it]
        yield
        for m_off in m_levels:
            t1 = [_mm(x[i], bd(a1[i][:T] * m_off)) for i in n_it]
            yield
            x = [x[i] - _mm(t1[i], bd(x[i])) for i in n_it]
            yield

        st = [st_ref[(b0 + b) * N_GROUPS + g] for b, g in items]
        ks = [jnp.dot(lhs[i], st[i].astype(BF16), preferred_element_type=F32) for i in n_it]
        yield
        sv = [ks[i] + _mm(a_k[i], bd(vv[i])) for i in n_it]
        yield
        u = [-_mm(x[i], bd(sv[i][:T])) for i in n_it]
        yield
        y_tiles = [sv[i][T:] + _mm(a_rb[i], bd(u[i])) for i in n_it]
        yield
        for i, (b, g) in enumerate(items):
            wk = jnp.concatenate([tile("bw", b, g), tile("kw", b, g)], axis=0)
            grow = _mm_tn(wk, jnp.concatenate([u[i].astype(BF16), vv[i]], axis=0)) * bdm32
            st_ref[(b0 + b) * N_GROUPS + g] = st[i] * w_col[(b, g)] + grow
            if i % 4 == 3:
                yield

        y = jnp.concatenate(
            [jnp.concatenate(y_tiles[b * N_GROUPS:(b + 1) * N_GROUPS], axis=1) for b in range(hb)],
            axis=0)
        inv_n = 1.0 / RWKV_HEAD
        mean = head_sum(y) * inv_n
        d = y - mean
        var = head_sum(d * d) * inv_n
        yield
        yn = d * lax.rsqrt(var + GN_EPS) * lnw_ref[...] + lnb_ref[...]
        out = ((yn + stage["bonus"][...]) * stage["gate"][...]).astype(o_ref.dtype)
        for b in range(hb):
            o_ref[b0 + b] = out[b * T:(b + 1) * T]

    @pl.when(pl.program_id(0) == 0)
    def _():
        st_ref[...] = jnp.zeros_like(st_ref)
        _interleave(prepare(x0_ref, stage_a))

    _interleave(recurrence(stage_a, 0), prepare(xb_ref, stage_b))
    _interleave(recurrence(stage_b, hb), prepare(xa_ref, stage_a))


def _rwkv(xx, w0, wdec_pad, a0, aup_pad, gup, k_k, k_a, r_k, lnx_w, lnx_b):
    B, S, _ = xx.shape
    W, G = RWKV_WIDTH, GROUP_LANES
    hb = B // 2
    n_chunks = S // CHUNK
    head = jnp.arange(G) // RWKV_HEAD
    seg = (head[:, None] == head[None, :]).astype(BF16)
    t = jnp.arange(hb * CHUNK)
    tril = ((t[:, None] >= t[None, :]) & (t[:, None] // CHUNK == t[None, :] // CHUNK)).astype(BF16)
    tmasks = _rwkv_tile_masks()
    vec = lambda n: _const_spec((1, n))
    half = lambda dtype: pltpu.VMEM((hb * CHUNK, W), dtype)
    stage = ([half(BF16)] * len(RWKV_PREP_BF16) + [half(F32)] * len(RWKV_PREP_F32)
             + [pltpu.VMEM((hb, 8, W), F32)])
    return pl.pallas_call(
        _rwkv_kernel,
        grid=(n_chunks,),
        in_specs=[pl.BlockSpec((hb, CHUNK, RWKV_COLS), lambda j: (0, 0, 0)),
                  pl.BlockSpec((hb, CHUNK, RWKV_COLS), lambda j: (0, jnp.minimum(j + 1, n_chunks - 1), 0)),
                  pl.BlockSpec((hb, CHUNK, RWKV_COLS), lambda j: (1, j, 0)),
                  vec(W), _const_spec((LANES, W)), vec(W), _const_spec((LANES, W)),
                  _const_spec((GATE_LORA, W)), vec(W), vec(W), vec(W), vec(W), vec(W),
                  _const_spec((G, G)), _const_spec((hb * CHUNK, hb * CHUNK)),
                  _const_spec(tmasks.shape)],
        out_specs=pl.BlockSpec((B, CHUNK, W), lambda j: (0, j, 0)),
        out_shape=jax.ShapeDtypeStruct((B, S, W), BF16),
        scratch_shapes=[pltpu.VMEM((B * N_GROUPS, G, G), F32)] + stage + stage,
        compiler_params=_params("arbitrary"),
        name="rwkv7",
    )(xx, xx, xx, w0, wdec_pad, a0, aup_pad, gup, k_k, k_a, r_k, lnx_w, lnx_b, seg, tril, tmasks)


def _diff_attn_kernel(lam_init, qt_in_ref, k_ref, vt_ref, lq1_ref, lk1_ref, lq2_ref, lk2_ref, sw_ref,
                      o_ref, qt_ref, m_ref, l_ref, acc_ref, sa_ref, sb_ref, ca_ref, cb_ref):
    tq = tk = ATTN_TILE
    D, H = DIFF_VDIM, DIFF_HEADS
    hs = range(H)
    i = pl.program_id(1)

    d_row = lax.broadcasted_iota(jnp.int32, (D, tq), 0)
    for h in hs:
        qt = qt_in_ref[0, 0, h * D:(h + 1) * D, :]
        zero = jnp.zeros_like(qt)
        qt_ref[h] = jnp.concatenate([jnp.where(d_row < DIFF_QKDIM, qt, zero),
                                     jnp.where(d_row >= DIFF_QKDIM, qt, zero)], axis=1)

    m_ref[...] = jnp.full(m_ref.shape, NEG_INF, F32)
    l_ref[...] = jnp.zeros(l_ref.shape, F32)
    acc_ref[...] = jnp.zeros(acc_ref.shape, F32)

    n_kc = tk // CHUNK

    def produce(t, buf):
        s_ref, cmax_ref = buf
        off = pl.multiple_of(t * tk, tk)
        for h in hs:
            s = jnp.dot(k_ref[0, pl.ds(off, tk), h * D:(h + 1) * D], qt_ref[h],
                        preferred_element_type=F32)
            s_ref[h] = s
            cmax_ref[h] = jnp.max(s.reshape(n_kc, CHUNK, 2 * tq), axis=1)

    def consume(t, buf, diagonal):
        s_ref, cmax_ref = buf
        cmax = [cmax_ref[h] for h in hs]
        s = [s_ref[h] for h in hs]
        if diagonal:
            qry_chunk = lambda shape: (lax.broadcasted_iota(jnp.int32, shape, 1) % tq) // CHUNK
            chunk_ok = lax.broadcasted_iota(jnp.int32, (n_kc, 2 * tq), 0) <= qry_chunk((n_kc, 2 * tq))
            key_ok = (lax.broadcasted_iota(jnp.int32, (tk, 2 * tq), 0) // CHUNK) <= qry_chunk((tk, 2 * tq))
            cmax = [jnp.where(chunk_ok, cmax[h], NEG_INF) for h in hs]
            s = [jnp.where(key_ok, s[h], NEG_INF) for h in hs]
        m_old = [m_ref[h] for h in hs]
        m_new = [jnp.maximum(m_old[h], jnp.max(cmax[h], axis=0, keepdims=True)) for h in hs]
        alpha = [jnp.exp2(m_old[h] - m_new[h]) for h in hs]
        p = [jnp.exp2(s[h] - m_new[h]) for h in hs]
        pv = [jnp.dot(vt_ref[0, t, h * D:(h + 1) * D, :], p[h].astype(BF16),
                      preferred_element_type=F32) for h in hs]
        for h in hs:
            m_ref[h] = m_new[h]
            l_ref[h] = alpha[h] * l_ref[h] + jnp.sum(p[h], axis=0, keepdims=True)
            acc_ref[h] = alpha[h] * acc_ref[h] + pv[h]

    def finish(buf):
        consume(i, buf, True)
        lam = (jnp.exp(jnp.sum(lq1_ref[...] * lk1_ref[...], axis=-1, keepdims=True))
               - jnp.exp(jnp.sum(lq2_ref[...] * lk2_ref[...], axis=-1, keepdims=True)) + lam_init)
        outs = []
        for h in hs:
            o2 = acc_ref[h] / l_ref[h]
            o = o2[:, :tq] - lam * o2[:, tq:]
            o = o * lax.rsqrt(jnp.mean(o * o, axis=0, keepdims=True) + RMS_EPS) * sw_ref[...]
            outs.append((o * (1.0 - lam_init)).T)
        o_ref[0] = jnp.concatenate(outs, axis=1).astype(o_ref.dtype)

    buf_a, buf_b = (sa_ref, ca_ref), (sb_ref, cb_ref)
    produce(0, buf_a)

    def pair(pi, carry):
        t = 2 * pi
        produce(t + 1, buf_b)
        consume(t, buf_a, False)
        produce(t + 2, buf_a)
        consume(t + 1, buf_b, False)
        return carry

    lax.fori_loop(0, i // 2, pair, 0)

    @pl.when(i % 2 == 1)
    def _():
        produce(i, buf_b)
        consume(i - 1, buf_a, False)
        finish(buf_b)

    @pl.when(i % 2 == 0)
    def _():
        finish(buf_a)


def _diff_attn(q_t, k, v_t, lam_q1, lam_k1, lam_q2, lam_k2, subln_col, lam_init):
    B, S, _ = k.shape
    H, D, tq = DIFF_HEADS, DIFF_VDIM, ATTN_TILE
    lamv = _const_spec((1, DIFF_QKDIM))
    scores = pltpu.VMEM((H, tq, 2 * tq), F32)
    chunk_max = pltpu.VMEM((H, tq // CHUNK, 2 * tq), F32)
    return pl.pallas_call(
        functools.partial(_diff_attn_kernel, lam_init),
        grid=(B, S // tq),
        in_specs=[pl.BlockSpec((1, 1, DIFF_WIDTH, tq), lambda b, i: (b, i, 0, 0)),
                  pl.BlockSpec((1, S, DIFF_WIDTH), lambda b, i: (b, 0, 0)),
                  pl.BlockSpec((1, S // tq, DIFF_WIDTH, tq), lambda b, i: (b, 0, 0, 0)),
                  lamv, lamv, lamv, lamv, _const_spec((D, 1))],
        out_specs=pl.BlockSpec((1, tq, DIFF_WIDTH), lambda b, i: (b, i, 0)),
        out_shape=jax.ShapeDtypeStruct((B, S, DIFF_WIDTH), BF16),
        scratch_shapes=[pltpu.VMEM((H, D, 2 * tq), BF16),
                        pltpu.VMEM((H, 1, 2 * tq), F32), pltpu.VMEM((H, 1, 2 * tq), F32),
                        pltpu.VMEM((H, D, 2 * tq), F32), scores, scores, chunk_max, chunk_max],
        compiler_params=_params("parallel", "arbitrary"),
        name="diff_attn",
    )(q_t, k, v_t, lam_q1, lam_k1, lam_q2, lam_k2, subln_col)


def _mem_kv_kernel(m_ref, nw_ref, wk_ref, wv_ref, k_ref, v_ref):
    mn = _rms(m_ref[...], nw_ref[...]).astype(BF16)
    k_ref[...] = jnp.dot(mn, wk_ref[...].astype(BF16), preferred_element_type=F32).astype(BF16)
    v_ref[...] = jnp.dot(mn, wv_ref[...].astype(BF16), preferred_element_type=F32).astype(BF16)


def _mem_kv(mem2, norm_w, w_mk, w_mv):
    n = mem2.shape[0]
    wspec = _const_spec((D_MODEL, D_MODEL))
    rows = pl.BlockSpec((ROW_TILE, D_MODEL), lambda i: (i, 0))
    return pl.pallas_call(
        _mem_kv_kernel,
        grid=(n // ROW_TILE,),
        in_specs=[rows, _const_spec((1, D_MODEL)), wspec, wspec],
        out_specs=[rows, rows],
        out_shape=[jax.ShapeDtypeStruct((n, D_MODEL), BF16)] * 2,
        compiler_params=_params("parallel"),
        name="mem_kv",
    )(mem2, norm_w, w_mk, w_mv)


def _mix_mem_kernel(x_ref, yr_ref, yd_ref, wo_ref, nw_ref, wq_ref, k_ref, v_ref, wmo_ref, o_ref):
    h = (x_ref[0]
         + jnp.dot(yr_ref[0], wo_ref[:RWKV_WIDTH, :].astype(BF16), preferred_element_type=F32)
         + jnp.dot(yd_ref[0], wo_ref[RWKV_WIDTH:, :].astype(BF16), preferred_element_type=F32))
    hn = _rms(h, nw_ref[...]).astype(BF16)
    q = jnp.dot(hn, wq_ref[...].astype(BF16), preferred_element_type=F32) * (MEM_HEAD_DIM ** -0.5)
    q = q.astype(BF16)
    outs = []
    for hd in range(MEM_HEADS):
        sl = slice(hd * MEM_HEAD_DIM, (hd + 1) * MEM_HEAD_DIM)
        s = lax.dot_general(q[:, sl], k_ref[0, :, sl], (((1,), (1,)), ((), ())),
                            preferred_element_type=F32)
        pr = jnp.exp(s - jnp.max(s, axis=-1, keepdims=True))
        den = jnp.sum(pr, axis=-1, keepdims=True)
        outs.append(jnp.dot(pr.astype(BF16), v_ref[0, :, sl], preferred_element_type=F32) / den)
    o = jnp.concatenate(outs, axis=1).astype(BF16)
    o_ref[0] = h + jnp.dot(o, wmo_ref[...].astype(BF16), preferred_element_type=F32)


def _mix_mem(x, y_rwkv, y_diff, w_out, norm_w, w_mq, k_mem, v_mem, w_mo):
    B, S, D = x.shape
    M = k_mem.shape[1]
    wspec = _const_spec((D, D))
    rows = lambda w: pl.BlockSpec((1, ROW_TILE, w), lambda b, i: (b, i, 0))
    memspec = pl.BlockSpec((1, M, D), lambda b, i: (b, 0, 0))
    return pl.pallas_call(
        _mix_mem_kernel,
        grid=(B, S // ROW_TILE),
        in_specs=[rows(D), rows(RWKV_WIDTH), rows(DIFF_WIDTH), wspec, _const_spec((1, D)), wspec,
                  memspec, memspec, wspec],
        out_specs=rows(D),
        out_shape=jax.ShapeDtypeStruct((B, S, D), F32),
        compiler_params=_params("parallel", "parallel"),
        name="mix_mem",
    )(x, y_rwkv, y_diff, w_out, norm_w, w_mq, k_mem, v_mem, w_mo)


def _mlp_kernel(final_norm, h_ref, nw_ref, wu_ref, wd_ref, fw_ref, o_ref):
    h = h_ref[...]
    hn = _rms(h, nw_ref[...]).astype(BF16)
    acc = h
    for c in range(D_FF // FF_TILE):
        sl = slice(c * FF_TILE, (c + 1) * FF_TILE)
        u = jnp.maximum(jnp.dot(hn, wu_ref[:, sl].astype(BF16), preferred_element_type=F32), 0.0)
        acc = acc + jnp.dot((u * u).astype(BF16), wd_ref[sl, :].astype(BF16), preferred_element_type=F32)
    o_ref[...] = _rms(acc, fw_ref[...]) if final_norm else acc


def _mlp(h2, norm_w, w_up, w_down, final_w, final_norm):
    n, D = h2.shape
    rows = pl.BlockSpec((MLP_ROW_TILE, D), lambda i: (i, 0))
    return pl.pallas_call(
        functools.partial(_mlp_kernel, final_norm),
        grid=(n // MLP_ROW_TILE,),
        in_specs=[rows, _const_spec((1, D)), _const_spec((D, D_FF)), _const_spec((D_FF, D)),
                  _const_spec((1, D))],
        out_specs=rows,
        out_shape=jax.ShapeDtypeStruct((n, D), F32),
        compiler_params=_params("parallel"),
        name="mlp",
    )(h2, norm_w, w_up, w_down, final_w)


def kernel(x, mem, norm_mix_w, w_in, mu_shift, w_decay0, w_decay_up, a0, a_up, g_up, k_k, k_a, r_k,
           lnx_w, lnx_b, lam_q1, lam_k1, lam_q2, lam_k2, subln_w, w_out, norm_mem_w, norm_src_w,
           w_mq, w_mk, w_mv, w_mo, norm_mlp_w, w_up, w_down, norm_final_w):
    B, S, D = x.shape
    depth = norm_mix_w.shape[0]
    row = lambda t: t.reshape(1, -1)
    h = x
    for l in range(depth):
        lam_init = 0.8 - 0.6 * math.exp(-0.3 * l)
        zeros_lora = jnp.zeros((DECAY_LORA, RWKV_WIDTH), BF16)
        wdec_pad = jnp.concatenate([w_decay_up[l].astype(BF16), zeros_lora], axis=0)
        aup_pad = jnp.concatenate([zeros_lora, a_up[l].astype(BF16)], axis=0)

        n_cols = RWKV_COLS + 2 * DIFF_WIDTH
        w_qv_t = jnp.concatenate([w_in[l][:, RWKV_COLS:K_COL0], w_in[l][:, n_cols:]], axis=1).T
        p_rwkv, k_att, q_t, v_t = _in_proj(h.reshape(B * S, D), row(norm_mix_w[l]),
                                           w_in[l][:, :n_cols].astype(BF16), w_qv_t.astype(BF16),
                                           row(mu_shift[l]), B)
        y_rwkv = _rwkv(p_rwkv.reshape(B, S, RWKV_COLS), row(w_decay0[l]),
                       wdec_pad, row(a0[l]), aup_pad, g_up[l].astype(BF16), row(k_k[l]),
                       row(k_a[l]), row(r_k[l]), row(lnx_w[l]), row(lnx_b[l]))
        y_diff = _diff_attn(q_t, k_att.reshape(B, S, DIFF_WIDTH), v_t, row(lam_q1[l]), row(lam_k1[l]),
                            row(lam_q2[l]), row(lam_k2[l]), subln_w[l].reshape(-1, 1), lam_init)
        k_mem, v_mem = _mem_kv(mem.reshape(-1, D), row(norm_src_w[l]), w_mk[l], w_mv[l])
        M = mem.shape[1]
        h = _mix_mem(h, y_rwkv, y_diff, w_out[l], row(norm_mem_w[l]), w_mq[l],
                     k_mem.reshape(B, M, D), v_mem.reshape(B, M, D), w_mo[l])
        h = _mlp(h.reshape(B * S, D), row(norm_mlp_w[l]), w_up[l], w_down[l], row(norm_final_w),
                 l == depth - 1)
        h = h.reshape(B, S, D)
    return h
```

```python
import functools
import math

import jax
import jax.numpy as jnp
from jax import lax
from jax.experimental import pallas as pl
from jax.experimental.pallas import tpu as pltpu

F32 = jnp.float32
BF16 = jnp.bfloat16

D_MODEL = 1024
CHUNK = 64
RWKV_HEAD = 64
RWKV_WIDTH = 512
RWKV_HEADS = 8
DECAY_LORA = 64
AAA_LORA = 64
GATE_LORA = 128
RWKV_COLS = 3 * RWKV_WIDTH + DECAY_LORA + AAA_LORA + GATE_LORA
GN_EPS = 64e-5
DIFF_WIDTH = 512
DIFF_HEADS = 4
DIFF_VDIM = 128
DIFF_QKDIM = 64
DIFF_COLS = 3 * DIFF_WIDTH
D_IN_TOTAL = RWKV_COLS + DIFF_COLS
MEM_HEADS = 4
MEM_HEAD_DIM = 256
D_FF = 4 * D_MODEL
RMS_EPS = 1e-5
NEG_INF = -1e30
LOG2_E = 1.4426950408889634

VMEM_LIMIT_BYTES = 56 * 1024 * 1024

ROW_TILE = 1024
MLP_ROW_TILE = 512
ATTN_TILE = 512
FF_TILE = 1024


def _mm(a, b):
    return jnp.dot(a.astype(BF16), b.astype(BF16), preferred_element_type=F32)


def _mm_nt(a, b):
    return lax.dot_general(a.astype(BF16), b.astype(BF16), (((1,), (1,)), ((), ())),
                           preferred_element_type=F32)


def _mm_tn(a, b):
    return lax.dot_general(a.astype(BF16), b.astype(BF16), (((0,), (0,)), ((), ())),
                           preferred_element_type=F32)


def _rms(x, w, eps=RMS_EPS):
    return x * lax.rsqrt(jnp.mean(x * x, axis=-1, keepdims=True) + eps) * w


def _params(*sem, flags=None):
    return pltpu.CompilerParams(dimension_semantics=sem, vmem_limit_bytes=VMEM_LIMIT_BYTES,
                                flags=flags)


def _interleave(*gens):
    live = list(gens)
    while live:
        for gen in list(live):
            if next(gen, StopIteration) is StopIteration:
                live.remove(gen)


def _const_spec(shape):
    nd = len(shape)
    return pl.BlockSpec(shape, lambda *_: (0,) * nd, pipeline_mode=pl.Buffered(1))


K_COL0 = RWKV_COLS + DIFF_WIDTH


def _in_proj_kernel(tiles_per_seq, x_ref, nw_ref, w_ref, wqvt_ref, mu_ref, pr_ref, k_ref, qt_ref, vt_ref,
                    carry_ref):
    i = pl.program_id(0)
    rows = x_ref.shape[0]

    @pl.when(i == 0)
    def _():
        carry_ref[...] = jnp.zeros_like(carry_ref)

    xn_f32 = _rms(x_ref[...], nw_ref[...])
    xn = xn_f32.astype(BF16)
    p = jnp.dot(xn, w_ref[:, :RWKV_COLS], preferred_element_type=F32)
    last_prev = jnp.where(i % tiles_per_seq == 0, 0.0, carry_ref[...])
    first_row = lax.broadcasted_iota(jnp.int32, p.shape, 0) == 0
    p_prev = jnp.where(first_row, last_prev, pltpu.roll(p, 1, 0))
    carry_ref[...] = p[rows - 1:rows, :]
    pr_ref[...] = p + (p_prev - p) * mu_ref[...]
    k_ref[...] = jnp.dot(xn, w_ref[:, K_COL0:], preferred_element_type=F32).astype(BF16)
    qv_t = jnp.dot(wqvt_ref[...], xn_f32.T.astype(BF16), preferred_element_type=F32)
    qt_ref[0, 0] = (qv_t[:DIFF_WIDTH] * (DIFF_QKDIM ** -0.5 * LOG2_E)).astype(BF16)
    vt_ref[0, 0] = qv_t[DIFF_WIDTH:].astype(BF16)


def _in_proj(x2, norm_w, w_in_bf, w_qvt_bf, mu, batch):
    n = x2.shape[0]
    tiles_per_seq = n // batch // ATTN_TILE
    n_cols = RWKV_COLS + 2 * DIFF_WIDTH
    t_spec = pl.BlockSpec((1, 1, DIFF_WIDTH, ATTN_TILE),
                          lambda i: (i // tiles_per_seq, i % tiles_per_seq, 0, 0))
    t_shape = jax.ShapeDtypeStruct((batch, tiles_per_seq, DIFF_WIDTH, ATTN_TILE), BF16)
    return pl.pallas_call(
        functools.partial(_in_proj_kernel, tiles_per_seq),
        grid=(n // ATTN_TILE,),
        in_specs=[pl.BlockSpec((ATTN_TILE, D_MODEL), lambda i: (i, 0)),
                  _const_spec((1, D_MODEL)),
                  _const_spec((D_MODEL, n_cols)),
                  _const_spec((2 * DIFF_WIDTH, D_MODEL)),
                  _const_spec((1, RWKV_COLS))],
        out_specs=[pl.BlockSpec((ATTN_TILE, RWKV_COLS), lambda i: (i, 0)),
                   pl.BlockSpec((ATTN_TILE, DIFF_WIDTH), lambda i: (i, 0)),
                   t_spec, t_spec],
        out_shape=[jax.ShapeDtypeStruct((n, RWKV_COLS), F32),
                   jax.ShapeDtypeStruct((n, DIFF_WIDTH), BF16),
                   t_shape, t_shape],
        scratch_shapes=[pltpu.VMEM((1, RWKV_COLS), F32)],
        compiler_params=_params("arbitrary"),
        name="in_proj",
    )(x2, norm_w, w_in_bf, w_qvt_bf, mu)


GROUP_LANES = 256
HEADS_PER_GROUP = GROUP_LANES // RWKV_HEAD
N_GROUPS = RWKV_WIDTH // GROUP_LANES
LANES = 128
RWKV_PREP_BF16 = ("kd", "rd", "bi", "ki", "bw", "kw", "v")
RWKV_PREP_F32 = ("bonus", "gate")


def _rwkv_tile_masks():
    r = jnp.arange(CHUNK)[:, None]
    c = (jnp.arange(GROUP_LANES) % CHUNK)[None, :]
    ms = [r > c, r >= c, r == c, (r > c) & ((r // 8) == (c // 8))]
    size = 8
    while size < CHUNK:
        ms.append(((r // (2 * size)) == (c // (2 * size))) & ((r // size) > (c // size)))
        size *= 2
    return jnp.stack(ms).astype(F32)


def _rwkv_kernel(x0_ref, xa_ref, xb_ref, w0_ref, wdec_ref, a0_ref, aup_ref, gup_ref, kk_ref, ka_ref,
                 rk_ref, lnw_ref, lnb_ref, seg_ref, tril_ref, tmask_ref, o_ref, st_ref, *prep_refs):
    T, W, G = CHUNK, RWKV_WIDTH, GROUP_LANES
    hb = xa_ref.shape[0]
    n_names = len(RWKV_PREP_BF16) + len(RWKV_PREP_F32) + 1
    names = RWKV_PREP_BF16 + RWKV_PREP_F32 + ("e_last",)
    stage_a = dict(zip(names, prep_refs[:n_names]))
    stage_b = dict(zip(names, prep_refs[n_names:]))

    seg = seg_ref[...]

    def head_sum(x):
        xb = x.astype(BF16)
        return jnp.concatenate(
            [jnp.dot(xb[:, g * G:(g + 1) * G], seg, preferred_element_type=F32)
             for g in range(N_GROUPS)], axis=1)

    def prepare(x_ref, stage):
        xx = jnp.concatenate([x_ref[b] for b in range(hb)], axis=0)
        r = xx[:, 0:W]
        k = xx[:, W:2 * W]
        v = xx[:, 2 * W:3 * W]
        lora_da = xx[:, 3 * W:3 * W + DECAY_LORA + AAA_LORA]
        gd = xx[:, 3 * W + DECAY_LORA + AAA_LORA:]

        z = w0_ref[...] + _mm(jnp.tanh(lora_da), wdec_ref[...])
        yield
        softplus_neg = jnp.maximum(-z, 0.0) + jnp.log(1.0 + jnp.exp(-jnp.abs(z)))
        yield
        log_decay = -jnp.exp(-softplus_neg - 0.5)
        yield
        a = jax.nn.sigmoid(a0_ref[...] + _mm(lora_da, aup_ref[...]))
        yield
        stage["gate"][...] = _mm(jax.nn.sigmoid(gd), gup_ref[...])
        yield

        kk = k * kk_ref[...]
        kk = kk * lax.rsqrt(jnp.maximum(head_sum(kk * kk), 1e-24))
        yield
        k_mod = k * (1.0 + (a - 1.0) * ka_ref[...])
        b_vec = kk * a
        yield

        ld_hi = log_decay.astype(BF16)
        ld_lo = (log_decay - ld_hi.astype(F32)).astype(BF16)
        tril = tril_ref[...]
        cl = (jnp.dot(tril, ld_hi, preferred_element_type=F32)
              + jnp.dot(tril, ld_lo, preferred_element_type=F32))
        yield
        e_incl = jnp.exp(cl)
        yield
        e_inv = jnp.exp(-cl)
        yield
        bi = b_vec * e_inv
        stage["bi"][...] = bi.astype(BF16)
        yield
        ki = k_mod * e_inv
        stage["ki"][...] = ki.astype(BF16)
        yield
        stage["kd"][...] = (kk * jnp.exp(cl - log_decay)).astype(BF16)
        yield
        stage["rd"][...] = (r * e_incl).astype(BF16)
        stage["v"][...] = v.astype(BF16)
        yield
        stage["bonus"][...] = head_sum(r * k_mod * rk_ref[...]) * v
        yield
        for b in range(hb):
            rows = slice(b * T, (b + 1) * T)
            w_total = e_incl[b * T + T - 1:b * T + T, :]
            stage["bw"][rows, :] = (bi[rows] * w_total).astype(BF16)
            stage["kw"][rows, :] = (ki[rows] * w_total).astype(BF16)
            stage["e_last"][b] = jnp.broadcast_to(w_total, (8, W))
            if b % 2 == 1:
                yield

    bdm32 = seg.astype(F32)
    m_strict, m_incl, m_eye, m_blk8 = (tmask_ref[i] for i in range(4))
    m_levels = [tmask_ref[i] for i in range(4, tmask_ref.shape[0])]
    m_both = jnp.concatenate([m_strict, m_incl], axis=0)

    lane_in_vreg = lax.broadcasted_iota(jnp.int32, (T, G), 1) % LANES
    half_masks = [jnp.where((lane_in_vreg // RWKV_HEAD) == j, 1.0, 0.0).astype(BF16)
                  for j in range(LANES // RWKV_HEAD)]
    zero_vreg_cols = jnp.zeros((T, LANES), BF16)

    def bd(y):
        yb = y.astype(BF16)
        kept = [yb * m for m in half_masks]
        rows = []
        for h in range(HEADS_PER_GROUP):
            col, j = divmod(h * RWKV_HEAD, LANES)
            j //= RWKV_HEAD
            rows.append(jnp.concatenate(
                [kept[j][:, c * LANES:(c + 1) * LANES] if c == col else zero_vreg_cols
                 for c in range(G // LANES)], axis=1))
        return jnp.concatenate(rows, axis=0)

    def recurrence(stage, b0):
        items = [(b, g) for b in range(hb) for g in range(N_GROUPS)]
        n_it = range(len(items))
        tile = lambda name, b, g: stage[name][b * T:(b + 1) * T, g * G:(g + 1) * G]
        kd = [tile("kd", b, g) for b, g in items]
        bi = [tile("bi", b, g) for b, g in items]
        ki = [tile("ki", b, g) for b, g in items]
        vv = [tile("v", b, g) for b, g in items]

        w_col = {}
        for b in range(hb):
            col = jnp.broadcast_to(stage["e_last"][b][0:1, :], (LANES, W)).T
            for g in range(N_GROUPS):
                w_col[(b, g)] = jnp.concatenate([col[g * G:(g + 1) * G]] * (G // LANES), axis=1)

        lhs = [jnp.concatenate([kd[i], tile("rd", *items[i])], axis=0) for i in n_it]
        a1 = [_mm(lhs[i], bd(bi[i]).astype(F32).T) for i in n_it]
        yield
        a2 = [_mm(lhs[i], bd(ki[i]).astype(F32).T) for i in n_it]
        yield
        a_k = [a2[i] * m_both for i in n_it]
        a_rb = [a1[i][T:] * m_incl for i in n_it]
        yield

        n0 = [a1[i][:T] * m_blk8 for i in n_it]
        n2 = [_mm(n0[i], bd(n0[i])) for i in n_it]
        yield
        n4 = [_mm(n2[i], bd(n2[i])) for i in n_it]
        yield
        x = [_mm(m_eye - n0[i], bd(m_eye + n2[i])) for i in n_it]
        yield
        x = [_mm(x[i], bd(m_eye + n4[i])) for i in n_it]
        yield
        for m_off in m_levels:
            t1 = [_mm(x[i], bd(a1[i][:T] * m_off)) for i in n_it]
            yield
            x = [x[i] - _mm(t1[i], bd(x[i])) for i in n_it]
            yield

        st = [st_ref[(b0 + b) * N_GROUPS + g] for b, g in items]
        ks = [jnp.dot(lhs[i], st[i].astype(BF16), preferred_element_type=F32) for i in n_it]
        yield
        sv = [ks[i] + _mm(a_k[i], bd(vv[i])) for i in n_it]
        yield
        u = [-_mm(x[i], bd(sv[i][:T])) for i in n_it]
        yield
        y_tiles = [sv[i][T:] + _mm(a_rb[i], bd(u[i])) for i in n_it]
        yield
        for i, (b, g) in enumerate(items):
            wk = jnp.concatenate([tile("bw", b, g), tile("kw", b, g)], axis=0)
            grow = _mm_tn(wk, jnp.concatenate([u[i].astype(BF16), vv[i]], axis=0)) * bdm32
            st_ref[(b0 + b) * N_GROUPS + g] = st[i] * w_col[(b, g)] + grow
            if i % 4 == 3:
                yield

        y = jnp.concatenate(
            [jnp.concatenate(y_tiles[b * N_GROUPS:(b + 1) * N_GROUPS], axis=1) for b in range(hb)],
            axis=0)
        inv_n = 1.0 / RWKV_HEAD
        mean = head_sum(y) * inv_n
        d = y - mean
        var = head_sum(d * d) * inv_n
        yield
        yn = d * lax.rsqrt(var + GN_EPS) * lnw_ref[...] + lnb_ref[...]
        out = ((yn + stage["bonus"][...]) * stage["gate"][...]).astype(o_ref.dtype)
        for b in range(hb):
            o_ref[b0 + b] = out[b * T:(b + 1) * T]

    @pl.when(pl.program_id(0) == 0)
    def _():
        st_ref[...] = jnp.zeros_like(st_ref)
        _interleave(prepare(x0_ref, stage_a))

    _interleave(recurrence(stage_a, 0), prepare(xb_ref, stage_b))
    _interleave(recurrence(stage_b, hb), prepare(xa_ref, stage_a))


def _rwkv(xx, w0, wdec_pad, a0, aup_pad, gup, k_k, k_a, r_k, lnx_w, lnx_b):
    B, S, _ = xx.shape
    W, G = RWKV_WIDTH, GROUP_LANES
    hb = B // 2
    n_chunks = S // CHUNK
    head = jnp.arange(G) // RWKV_HEAD
    seg = (head[:, None] == head[None, :]).astype(BF16)
    t = jnp.arange(hb * CHUNK)
    tril = ((t[:, None] >= t[None, :]) & (t[:, None] // CHUNK == t[None, :] // CHUNK)).astype(BF16)
    tmasks = _rwkv_tile_masks()
    vec = lambda n: _const_spec((1, n))
    half = lambda dtype: pltpu.VMEM((hb * CHUNK, W), dtype)
    stage = ([half(BF16)] * len(RWKV_PREP_BF16) + [half(F32)] * len(RWKV_PREP_F32)
             + [pltpu.VMEM((hb, 8, W), F32)])
    return pl.pallas_call(
        _rwkv_kernel,
        grid=(n_chunks,),
        in_specs=[pl.BlockSpec((hb, CHUNK, RWKV_COLS), lambda j: (0, 0, 0)),
                  pl.BlockSpec((hb, CHUNK, RWKV_COLS), lambda j: (0, jnp.minimum(j + 1, n_chunks - 1), 0)),
                  pl.BlockSpec((hb, CHUNK, RWKV_COLS), lambda j: (1, j, 0)),
                  vec(W), _const_spec((LANES, W)), vec(W), _const_spec((LANES, W)),
                  _const_spec((GATE_LORA, W)), vec(W), vec(W), vec(W), vec(W), vec(W),
                  _const_spec((G, G)), _const_spec((hb * CHUNK, hb * CHUNK)),
                  _const_spec(tmasks.shape)],
        out_specs=pl.BlockSpec((B, CHUNK, W), lambda j: (0, j, 0)),
        out_shape=jax.ShapeDtypeStruct((B, S, W), BF16),
        scratch_shapes=[pltpu.VMEM((B * N_GROUPS, G, G), F32)] + stage + stage,
        compiler_params=_params("arbitrary"),
        name="rwkv7",
    )(xx, xx, xx, w0, wdec_pad, a0, aup_pad, gup, k_k, k_a, r_k, lnx_w, lnx_b, seg, tril, tmasks)


def _diff_attn_kernel(lam_init, qt_in_ref, k_ref, vt_ref, lq1_ref, lk1_ref, lq2_ref, lk2_ref, sw_ref,
                      o_ref, qt_ref, m_ref, l_ref, acc_ref, sa_ref, sb_ref, ca_ref, cb_ref):
    tq = tk = ATTN_TILE
    D, H = DIFF_VDIM, DIFF_HEADS
    hs = range(H)
    i = pl.program_id(1)

    d_row = lax.broadcasted_iota(jnp.int32, (D, tq), 0)
    for h in hs:
        qt = qt_in_ref[0, 0, h * D:(h + 1) * D, :]
        zero = jnp.zeros_like(qt)
        qt_ref[h] = jnp.concatenate([jnp.where(d_row < DIFF_QKDIM, qt, zero),
                                     jnp.where(d_row >= DIFF_QKDIM, qt, zero)], axis=1)

    m_ref[...] = jnp.full(m_ref.shape, NEG_INF, F32)
    l_ref[...] = jnp.zeros(l_ref.shape, F32)
    acc_ref[...] = jnp.zeros(acc_ref.shape, F32)

    n_kc = tk // CHUNK

    def produce(t, buf):
        s_ref, cmax_ref = buf
        off = pl.multiple_of(t * tk, tk)
        for h in hs:
            s = jnp.dot(k_ref[0, pl.ds(off, tk), h * D:(h + 1) * D], qt_ref[h],
                        preferred_element_type=F32)
            s_ref[h] = s
            cmax_ref[h] = jnp.max(s.reshape(n_kc, CHUNK, 2 * tq), axis=1)

    def consume(t, buf, diagonal):
        s_ref, cmax_ref = buf
        cmax = [cmax_ref[h] for h in hs]
        s = [s_ref[h] for h in hs]
        if diagonal:
            qry_chunk = lambda shape: (lax.broadcasted_iota(jnp.int32, shape, 1) % tq) // CHUNK
            chunk_ok = lax.broadcasted_iota(jnp.int32, (n_kc, 2 * tq), 0) <= qry_chunk((n_kc, 2 * tq))
            key_ok = (lax.broadcasted_iota(jnp.int32, (tk, 2 * tq), 0) // CHUNK) <= qry_chunk((tk, 2 * tq))
            cmax = [jnp.where(chunk_ok, cmax[h], NEG_INF) for h in hs]
            s = [jnp.where(key_ok, s[h], NEG_INF) for h in hs]
        m_old = [m_ref[h] for h in hs]
        m_new = [jnp.maximum(m_old[h], jnp.max(cmax[h], axis=0, keepdims=True)) for h in hs]
        alpha = [jnp.exp2(m_old[h] - m_new[h]) for h in hs]
        p = [jnp.exp2(s[h] - m_new[h]) for h in hs]
        pv = [jnp.dot(vt_ref[0, t, h * D:(h + 1) * D, :], p[h].astype(BF16),
                      preferred_element_type=F32) for h in hs]
        for h in hs:
            m_ref[h] = m_new[h]
            l_ref[h] = alpha[h] * l_ref[h] + jnp.sum(p[h], axis=0, keepdims=True)
            acc_ref[h] = alpha[h] * acc_ref[h] + pv[h]

    def finish(buf):
        consume(i, buf, True)
        lam = (jnp.exp(jnp.sum(lq1_ref[...] * lk1_ref[...], axis=-1, keepdims=True))
               - jnp.exp(jnp.sum(lq2_ref[...] * lk2_ref[...], axis=-1, keepdims=True)) + lam_init)
        outs = []
        for h in hs:
            o2 = acc_ref[h] / l_ref[h]
            o = o2[:, :tq] - lam * o2[:, tq:]
            o = o * lax.rsqrt(jnp.mean(o * o, axis=0, keepdims=True) + RMS_EPS) * sw_ref[...]
            outs.append((o * (1.0 - lam_init)).T)
        o_ref[0] = jnp.concatenate(outs, axis=1).astype(o_ref.dtype)

    buf_a, buf_b = (sa_ref, ca_ref), (sb_ref, cb_ref)
    produce(0, buf_a)

    def pair(pi, carry):
        t = 2 * pi
        produce(t + 1, buf_b)
        consume(t, buf_a, False)
        produce(t + 2, buf_a)
        consume(t + 1, buf_b, False)
        return carry

    lax.fori_loop(0, i // 2, pair, 0)

    @pl.when(i % 2 == 1)
    def _():
        produce(i, buf_b)
        consume(i - 1, buf_a, False)
        finish(buf_b)

    @pl.when(i % 2 == 0)
    def _():
        finish(buf_a)


def _diff_attn(q_t, k, v_t, lam_q1, lam_k1, lam_q2, lam_k2, subln_col, lam_init):
    B, S, _ = k.shape
    H, D, tq = DIFF_HEADS, DIFF_VDIM, ATTN_TILE
    lamv = _const_spec((1, DIFF_QKDIM))
    scores = pltpu.VMEM((H, tq, 2 * tq), F32)
    chunk_max = pltpu.VMEM((H, tq // CHUNK, 2 * tq), F32)
    return pl.pallas_call(
        functools.partial(_diff_attn_kernel, lam_init),
        grid=(B, S // tq),
        in_specs=[pl.BlockSpec((1, 1, DIFF_WIDTH, tq), lambda b, i: (b, i, 0, 0)),
                  pl.BlockSpec((1, S, DIFF_WIDTH), lambda b, i: (b, 0, 0)),
                  pl.BlockSpec((1, S // tq, DIFF_WIDTH, tq), lambda b, i: (b, 0, 0, 0)),
                  lamv, lamv, lamv, lamv, _const_spec((D, 1))],
        out_specs=pl.BlockSpec((1, tq, DIFF_WIDTH), lambda b, i: (b, i, 0)),
        out_shape=jax.ShapeDtypeStruct((B, S, DIFF_WIDTH), BF16),
        scratch_shapes=[pltpu.VMEM((H, D, 2 * tq), BF16),
                        pltpu.VMEM((H, 1, 2 * tq), F32), pltpu.VMEM((H, 1, 2 * tq), F32),
                        pltpu.VMEM((H, D, 2 * tq), F32), scores, scores, chunk_max, chunk_max],
        compiler_params=_params("parallel", "arbitrary"),
        name="diff_attn",
    )(q_t, k, v_t, lam_q1, lam_k1, lam_q2, lam_k2, subln_col)


def _mem_kv_kernel(m_ref, nw_ref, wk_ref, wv_ref, kt_ref, v_ref):
    nb, mem_tokens = v_ref.shape[0], v_ref.shape[1]
    mn = _rms(m_ref[...], nw_ref[...]).astype(BF16)
    k = jnp.dot(mn, wk_ref[...].astype(BF16), preferred_element_type=F32)
    v = jnp.dot(mn, wv_ref[...].astype(BF16), preferred_element_type=F32).astype(BF16)
    for b in range(nb):
        rows = slice(b * mem_tokens, (b + 1) * mem_tokens)
        kt_ref[b] = k[rows].T.astype(BF16)
        v_ref[b] = v[rows]


def _mem_kv(mem, norm_w, w_mk, w_mv):
    B, M, D = mem.shape
    nb = ROW_TILE // M
    wspec = _const_spec((D, D))
    return pl.pallas_call(
        _mem_kv_kernel,
        grid=(B // nb,),
        in_specs=[pl.BlockSpec((nb * M, D), lambda i: (i, 0)), _const_spec((1, D)), wspec, wspec],
        out_specs=[pl.BlockSpec((nb, D, M), lambda i: (i, 0, 0)),
                   pl.BlockSpec((nb, M, D), lambda i: (i, 0, 0))],
        out_shape=[jax.ShapeDtypeStruct((B, D, M), BF16), jax.ShapeDtypeStruct((B, M, D), BF16)],
        compiler_params=_params("parallel"),
        name="mem_kv",
    )(mem.reshape(B * M, D), norm_w, w_mk, w_mv)


def _mix_mem_kernel(x_ref, yr_ref, yd_ref, wo_ref, nw_ref, wq_ref, kt_ref, v_ref, wmo_ref, o_ref):
    h = (x_ref[0]
         + jnp.dot(yr_ref[0], wo_ref[:RWKV_WIDTH, :].astype(BF16), preferred_element_type=F32)
         + jnp.dot(yd_ref[0], wo_ref[RWKV_WIDTH:, :].astype(BF16), preferred_element_type=F32))
    hn = _rms(h, nw_ref[...]).astype(BF16)
    q = jnp.dot(hn, wq_ref[...].astype(BF16), preferred_element_type=F32) * (MEM_HEAD_DIM ** -0.5)
    q = q.astype(BF16)
    outs = []
    for hd in range(MEM_HEADS):
        sl = slice(hd * MEM_HEAD_DIM, (hd + 1) * MEM_HEAD_DIM)
        s = jnp.dot(q[:, sl], kt_ref[0, sl, :], preferred_element_type=F32)
        pr = jnp.exp(s - jnp.max(s, axis=-1, keepdims=True))
        den = jnp.sum(pr, axis=-1, keepdims=True)
        outs.append(jnp.dot(pr.astype(BF16), v_ref[0, :, sl], preferred_element_type=F32) / den)
    o = jnp.concatenate(outs, axis=1).astype(BF16)
    o_ref[0] = h + jnp.dot(o, wmo_ref[...].astype(BF16), preferred_element_type=F32)


def _mix_mem(x, y_rwkv, y_diff, w_out, norm_w, w_mq, kt_mem, v_mem, w_mo):
    B, S, D = x.shape
    M = v_mem.shape[1]
    wspec = _const_spec((D, D))
    rows = lambda w: pl.BlockSpec((1, ROW_TILE, w), lambda b, i: (b, i, 0))
    memspec = pl.BlockSpec((1, M, D), lambda b, i: (b, 0, 0))
    return pl.pallas_call(
        _mix_mem_kernel,
        grid=(B, S // ROW_TILE),
        in_specs=[rows(D), rows(RWKV_WIDTH), rows(DIFF_WIDTH), wspec, _const_spec((1, D)), wspec,
                  pl.BlockSpec((1, D, M), lambda b, i: (b, 0, 0)), memspec, wspec],
        out_specs=rows(D),
        out_shape=jax.ShapeDtypeStruct((B, S, D), F32),
        compiler_params=_params("parallel", "parallel"),
        name="mix_mem",
    )(x, y_rwkv, y_diff, w_out, norm_w, w_mq, kt_mem, v_mem, w_mo)


def _mlp_kernel(final_norm, h_ref, nw_ref, wu_ref, wd_ref, fw_ref, o_ref):
    h = h_ref[...]
    hn = _rms(h, nw_ref[...]).astype(BF16)
    acc = h
    for c in range(D_FF // FF_TILE):
        sl = slice(c * FF_TILE, (c + 1) * FF_TILE)
        u = jnp.maximum(jnp.dot(hn, wu_ref[:, sl].astype(BF16), preferred_element_type=F32), 0.0)
        acc = acc + jnp.dot((u * u).astype(BF16), wd_ref[sl, :].astype(BF16), preferred_element_type=F32)
    o_ref[...] = _rms(acc, fw_ref[...]) if final_norm else acc


def _mlp(h2, norm_w, w_up, w_down, final_w, final_norm):
    n, D = h2.shape
    rows = pl.BlockSpec((MLP_ROW_TILE, D), lambda i: (i, 0))
    return pl.pallas_call(
        functools.partial(_mlp_kernel, final_norm),
        grid=(n // MLP_ROW_TILE,),
        in_specs=[rows, _const_spec((1, D)), _const_spec((D, D_FF)), _const_spec((D_FF, D)),
                  _const_spec((1, D))],
        out_specs=rows,
        out_shape=jax.ShapeDtypeStruct((n, D), F32),
        compiler_params=_params("parallel"),
        name="mlp",
    )(h2, norm_w, w_up, w_down, final_w)


def kernel(x, mem, norm_mix_w, w_in, mu_shift, w_decay0, w_decay_up, a0, a_up, g_up, k_k, k_a, r_k,
           lnx_w, lnx_b, lam_q1, lam_k1, lam_q2, lam_k2, subln_w, w_out, norm_mem_w, norm_src_w,
           w_mq, w_mk, w_mv, w_mo, norm_mlp_w, w_up, w_down, norm_final_w):
    B, S, D = x.shape
    depth = norm_mix_w.shape[0]
    row = lambda t: t.reshape(1, -1)
    h = x
    for l in range(depth):
        lam_init = 0.8 - 0.6 * math.exp(-0.3 * l)
        zeros_lora = jnp.zeros((DECAY_LORA, RWKV_WIDTH), BF16)
        wdec_pad = jnp.concatenate([w_decay_up[l].astype(BF16), zeros_lora], axis=0)
        aup_pad = jnp.concatenate([zeros_lora, a_up[l].astype(BF16)], axis=0)

        n_cols = RWKV_COLS + 2 * DIFF_WIDTH
        w_qv_t = jnp.concatenate([w_in[l][:, RWKV_COLS:K_COL0], w_in[l][:, n_cols:]], axis=1).T
        p_rwkv, k_att, q_t, v_t = _in_proj(h.reshape(B * S, D), row(norm_mix_w[l]),
                                           w_in[l][:, :n_cols].astype(BF16), w_qv_t.astype(BF16),
                                           row(mu_shift[l]), B)
        y_rwkv = _rwkv(p_rwkv.reshape(B, S, RWKV_COLS), row(w_decay0[l]),
                       wdec_pad, row(a0[l]), aup_pad, g_up[l].astype(BF16), row(k_k[l]),
                       row(k_a[l]), row(r_k[l]), row(lnx_w[l]), row(lnx_b[l]))
        y_diff = _diff_attn(q_t, k_att.reshape(B, S, DIFF_WIDTH), v_t, row(lam_q1[l]), row(lam_k1[l]),
                            row(lam_q2[l]), row(lam_k2[l]), subln_w[l].reshape(-1, 1), lam_init)
        kt_mem, v_mem = _mem_kv(mem, row(norm_src_w[l]), w_mk[l], w_mv[l])
        h = _mix_mem(h, y_rwkv, y_diff, w_out[l], row(norm_mem_w[l]), w_mq[l], kt_mem, v_mem, w_mo[l])
        h = _mlp(h.reshape(B * S, D), row(norm_mlp_w[l]), w_up[l], w_down[l], row(norm_final_w),
                 l == depth - 1)
        h = h.reshape(B, S, D)
    return h
```

```python
import functools
import math

import jax
import jax.numpy as jnp
from jax import lax
from jax.experimental import pallas as pl
from jax.experimental.pallas import tpu as pltpu

F32 = jnp.float32
BF16 = jnp.bfloat16

D_MODEL = 1024
CHUNK = 64
RWKV_HEAD = 64
RWKV_WIDTH = 512
RWKV_HEADS = 8
DECAY_LORA = 64
AAA_LORA = 64
GATE_LORA = 128
RWKV_COLS = 3 * RWKV_WIDTH + DECAY_LORA + AAA_LORA + GATE_LORA
GN_EPS = 64e-5
DIFF_WIDTH = 512
DIFF_HEADS = 4
DIFF_VDIM = 128
DIFF_QKDIM = 64
DIFF_COLS = 3 * DIFF_WIDTH
D_IN_TOTAL = RWKV_COLS + DIFF_COLS
MEM_HEADS = 4
MEM_HEAD_DIM = 256
D_FF = 4 * D_MODEL
RMS_EPS = 1e-5
NEG_INF = -1e30
LOG2_E = 1.4426950408889634

VMEM_LIMIT_BYTES = 56 * 1024 * 1024

ROW_TILE = 1024
MLP_ROW_TILE = 512
ATTN_TILE = 512
FF_TILE = 1024


def _mm(a, b):
    return jnp.dot(a.astype(BF16), b.astype(BF16), preferred_element_type=F32)


def _mm_nt(a, b):
    return lax.dot_general(a.astype(BF16), b.astype(BF16), (((1,), (1,)), ((), ())),
                           preferred_element_type=F32)


def _mm_tn(a, b):
    return lax.dot_general(a.astype(BF16), b.astype(BF16), (((0,), (0,)), ((), ())),
                           preferred_element_type=F32)


def _rms(x, w, eps=RMS_EPS):
    return x * lax.rsqrt(jnp.mean(x * x, axis=-1, keepdims=True) + eps) * w


def _params(*sem, flags=None):
    return pltpu.CompilerParams(dimension_semantics=sem, vmem_limit_bytes=VMEM_LIMIT_BYTES,
                                flags=flags)


def _interleave(*gens):
    live = list(gens)
    while live:
        for gen in list(live):
            if next(gen, StopIteration) is StopIteration:
                live.remove(gen)


def _const_spec(shape):
    nd = len(shape)
    return pl.BlockSpec(shape, lambda *_: (0,) * nd, pipeline_mode=pl.Buffered(1))


K_COL0 = RWKV_COLS + DIFF_WIDTH


def _in_proj_kernel(tiles_per_seq, x_ref, nw_ref, w_ref, wqvt_ref, mu_ref, pr_ref, k_ref, qt_ref, vt_ref,
                    carry_ref):
    i = pl.program_id(0)
    rows = x_ref.shape[0]

    @pl.when(i == 0)
    def _():
        carry_ref[...] = jnp.zeros_like(carry_ref)

    xn = _rms(x_ref[...], nw_ref[...]).astype(BF16)
    p = jnp.dot(xn, w_ref[:, :RWKV_COLS], preferred_element_type=F32)
    last_prev = jnp.where(i % tiles_per_seq == 0, 0.0, carry_ref[...])
    first_row = lax.broadcasted_iota(jnp.int32, p.shape, 0) == 0
    p_prev = jnp.where(first_row, last_prev, pltpu.roll(p, 1, 0))
    carry_ref[...] = p[rows - 1:rows, :]
    pr_ref[...] = p + (p_prev - p) * mu_ref[...]
    k_ref[...] = jnp.dot(xn, w_ref[:, K_COL0:], preferred_element_type=F32).astype(BF16)
    qv_t = lax.dot_general(wqvt_ref[...], xn, (((1,), (1,)), ((), ())), preferred_element_type=F32)
    qt_ref[0, 0] = (qv_t[:DIFF_WIDTH] * (DIFF_QKDIM ** -0.5 * LOG2_E)).astype(BF16)
    vt_ref[0, 0] = qv_t[DIFF_WIDTH:].astype(BF16)


def _in_proj(x2, norm_w, w_in_bf, w_qvt_bf, mu, batch):
    n = x2.shape[0]
    tiles_per_seq = n // batch // ATTN_TILE
    n_cols = RWKV_COLS + 2 * DIFF_WIDTH
    t_spec = pl.BlockSpec((1, 1, DIFF_WIDTH, ATTN_TILE),
                          lambda i: (i // tiles_per_seq, i % tiles_per_seq, 0, 0))
    t_shape = jax.ShapeDtypeStruct((batch, tiles_per_seq, DIFF_WIDTH, ATTN_TILE), BF16)
    return pl.pallas_call(
        functools.partial(_in_proj_kernel, tiles_per_seq),
        grid=(n // ATTN_TILE,),
        in_specs=[pl.BlockSpec((ATTN_TILE, D_MODEL), lambda i: (i, 0)),
                  _const_spec((1, D_MODEL)),
                  _const_spec((D_MODEL, n_cols)),
                  _const_spec((2 * DIFF_WIDTH, D_MODEL)),
                  _const_spec((1, RWKV_COLS))],
        out_specs=[pl.BlockSpec((ATTN_TILE, RWKV_COLS), lambda i: (i, 0)),
                   pl.BlockSpec((ATTN_TILE, DIFF_WIDTH), lambda i: (i, 0)),
                   t_spec, t_spec],
        out_shape=[jax.ShapeDtypeStruct((n, RWKV_COLS), F32),
                   jax.ShapeDtypeStruct((n, DIFF_WIDTH), BF16),
                   t_shape, t_shape],
        scratch_shapes=[pltpu.VMEM((1, RWKV_COLS), F32)],
        compiler_params=_params("arbitrary"),
        name="in_proj",
    )(x2, norm_w, w_in_bf, w_qvt_bf, mu)


GROUP_LANES = 256
HEADS_PER_GROUP = GROUP_LANES // RWKV_HEAD
N_GROUPS = RWKV_WIDTH // GROUP_LANES
LANES = 128
RWKV_PREP_BF16 = ("kd", "rd", "bi", "ki", "bw", "kw", "v")
RWKV_PREP_F32 = ("bonus", "gate")


def _rwkv_tile_masks():
    r = jnp.arange(CHUNK)[:, None]
    c = (jnp.arange(GROUP_LANES) % CHUNK)[None, :]
    ms = [r > c, r >= c, r == c, (r > c) & ((r // 8) == (c // 8))]
    size = 8
    while size < CHUNK:
        ms.append(((r // (2 * size)) == (c // (2 * size))) & ((r // size) > (c // size)))
        size *= 2
    return jnp.stack(ms).astype(F32)


def _rwkv_kernel(x0_ref, xa_ref, xb_ref, w0_ref, wdec_ref, a0_ref, aup_ref, gup_ref, kk_ref, ka_ref,
                 rk_ref, lnw_ref, lnb_ref, seg_ref, tril_ref, tmask_ref, o_ref, st_ref, *prep_refs):
    T, W, G = CHUNK, RWKV_WIDTH, GROUP_LANES
    hb = xa_ref.shape[0]
    n_names = len(RWKV_PREP_BF16) + len(RWKV_PREP_F32) + 1
    names = RWKV_PREP_BF16 + RWKV_PREP_F32 + ("e_last",)
    stage_a = dict(zip(names, prep_refs[:n_names]))
    stage_b = dict(zip(names, prep_refs[n_names:]))

    seg = seg_ref[...]

    def head_sum(x):
        xb = x.astype(BF16)
        return jnp.concatenate(
            [jnp.dot(xb[:, g * G:(g + 1) * G], seg, preferred_element_type=F32)
             for g in range(N_GROUPS)], axis=1)

    def prepare(x_ref, stage):
        xx = jnp.concatenate([x_ref[b] for b in range(hb)], axis=0)
        r = xx[:, 0:W]
        k = xx[:, W:2 * W]
        v = xx[:, 2 * W:3 * W]
        lora_da = xx[:, 3 * W:3 * W + DECAY_LORA + AAA_LORA]
        gd = xx[:, 3 * W + DECAY_LORA + AAA_LORA:]

        z = w0_ref[...] + _mm(jnp.tanh(lora_da), wdec_ref[...])
        yield
        softplus_neg = jnp.maximum(-z, 0.0) + jnp.log(1.0 + jnp.exp(-jnp.abs(z)))
        yield
        log_decay = -jnp.exp(-softplus_neg - 0.5)
        yield
        a = jax.nn.sigmoid(a0_ref[...] + _mm(lora_da, aup_ref[...]))
        yield
        stage["gate"][...] = _mm(jax.nn.sigmoid(gd), gup_ref[...])
        yield

        kk = k * kk_ref[...]
        kk = kk * lax.rsqrt(jnp.maximum(head_sum(kk * kk), 1e-24))
        yield
        k_mod = k * (1.0 + (a - 1.0) * ka_ref[...])
        b_vec = kk * a
        yield

        ld_hi = log_decay.astype(BF16)
        ld_lo = (log_decay - ld_hi.astype(F32)).astype(BF16)
        tril = tril_ref[...]
        cl = (jnp.dot(tril, ld_hi, preferred_element_type=F32)
              + jnp.dot(tril, ld_lo, preferred_element_type=F32))
        yield
        e_incl = jnp.exp(cl)
        yield
        e_inv = jnp.exp(-cl)
        yield
        bi = b_vec * e_inv
        stage["bi"][...] = bi.astype(BF16)
        yield
        ki = k_mod * e_inv
        stage["ki"][...] = ki.astype(BF16)
        yield
        stage["kd"][...] = (kk * jnp.exp(cl - log_decay)).astype(BF16)
        yield
        stage["rd"][...] = (r * e_incl).astype(BF16)
        stage["v"][...] = v.astype(BF16)
        yield
        stage["bonus"][...] = head_sum(r * k_mod * rk_ref[...]) * v
        yield
        for b in range(hb):
            rows = slice(b * T, (b + 1) * T)
            w_total = e_incl[b * T + T - 1:b * T + T, :]
            stage["bw"][rows, :] = (bi[rows] * w_total).astype(BF16)
            stage["kw"][rows, :] = (ki[rows] * w_total).astype(BF16)
            stage["e_last"][b] = jnp.broadcast_to(w_total, (8, W))
            if b % 2 == 1:
                yield

    bdm32 = seg.astype(F32)
    m_strict, m_incl, m_eye, m_blk8 = (tmask_ref[i] for i in range(4))
    m_levels = [tmask_ref[i] for i in range(4, tmask_ref.shape[0])]
    m_both = jnp.concatenate([m_strict, m_incl], axis=0)

    lane_in_vreg = lax.broadcasted_iota(jnp.int32, (T, G), 1) % LANES
    half_masks = [jnp.where((lane_in_vreg // RWKV_HEAD) == j, 1.0, 0.0).astype(BF16)
                  for j in range(LANES // RWKV_HEAD)]
    zero_vreg_cols = jnp.zeros((T, LANES), BF16)

    def bd(y):
        yb = y.astype(BF16)
        kept = [yb * m for m in half_masks]
        rows = []
        for h in range(HEADS_PER_GROUP):
            col, j = divmod(h * RWKV_HEAD, LANES)
            j //= RWKV_HEAD
            rows.append(jnp.concatenate(
                [kept[j][:, c * LANES:(c + 1) * LANES] if c == col else zero_vreg_cols
                 for c in range(G // LANES)], axis=1))
        return jnp.concatenate(rows, axis=0)

    def recurrence(stage, b0):
        items = [(b, g) for b in range(hb) for g in range(N_GROUPS)]
        n_it = range(len(items))
        tile = lambda name, b, g: stage[name][b * T:(b + 1) * T, g * G:(g + 1) * G]
        kd = [tile("kd", b, g) for b, g in items]
        bi = [tile("bi", b, g) for b, g in items]
        ki = [tile("ki", b, g) for b, g in items]
        vv = [tile("v", b, g) for b, g in items]

        w_col = {}
        for b in range(hb):
            col = jnp.broadcast_to(stage["e_last"][b][0:1, :], (LANES, W)).T
            for g in range(N_GROUPS):
                w_col[(b, g)] = jnp.concatenate([col[g * G:(g + 1) * G]] * (G // LANES), axis=1)

        lhs = [jnp.concatenate([kd[i], tile("rd", *items[i])], axis=0) for i in n_it]
        a1 = [_mm(lhs[i], bd(bi[i]).astype(F32).T) for i in n_it]
        yield
        a2 = [_mm(lhs[i], bd(ki[i]).astype(F32).T) for i in n_it]
        yield
        a_k = [a2[i] * m_both for i in n_it]
        a_rb = [a1[i][T:] * m_incl for i in n_it]
        yield

        n0 = [a1[i][:T] * m_blk8 for i in n_it]
        n2 = [_mm(n0[i], bd(n0[i])) for i in n_it]
        yield
        n4 = [_mm(n2[i], bd(n2[i])) for i in n_it]
        yield
        x = [_mm(m_eye - n0[i], bd(m_eye + n2[i])) for i in n_it]
        yield
        x = [_mm(x[i], bd(m_eye + n4[i])) for i in n_it]
        yield
        for m_off in m_levels:
            t1 = [_mm(x[i], bd(a1[i][:T] * m_off)) for i in n_it]
            yield
            x = [x[i] - _mm(t1[i], bd(x[i])) for i in n_it]
            yield

        st = [st_ref[(b0 + b) * N_GROUPS + g] for b, g in items]
        ks = [jnp.dot(lhs[i], st[i].astype(BF16), preferred_element_type=F32) for i in n_it]
        yield
        sv = [ks[i] + _mm(a_k[i], bd(vv[i])) for i in n_it]
        yield
        u = [-_mm(x[i], bd(sv[i][:T])) for i in n_it]
        yield
        y_tiles = [sv[i][T:] + _mm(a_rb[i], bd(u[i])) for i in n_it]
        yield
        for i, (b, g) in enumerate(items):
            wk = jnp.concatenate([tile("bw", b, g), tile("kw", b, g)], axis=0)
            grow = _mm_tn(wk, jnp.concatenate([u[i].astype(BF16), vv[i]], axis=0)) * bdm32
            st_ref[(b0 + b) * N_GROUPS + g] = st[i] * w_col[(b, g)] + grow
            if i % 4 == 3:
                yield

        y = jnp.concatenate(
            [jnp.concatenate(y_tiles[b * N_GROUPS:(b + 1) * N_GROUPS], axis=1) for b in range(hb)],
            axis=0)
        inv_n = 1.0 / RWKV_HEAD
        mean = head_sum(y) * inv_n
        d = y - mean
        var = head_sum(d * d) * inv_n
        yield
        yn = d * lax.rsqrt(var + GN_EPS) * lnw_ref[...] + lnb_ref[...]
        out = ((yn + stage["bonus"][...]) * stage["gate"][...]).astype(o_ref.dtype)
        for b in range(hb):
            o_ref[b0 + b] = out[b * T:(b + 1) * T]

    @pl.when(pl.program_id(0) == 0)
    def _():
        st_ref[...] = jnp.zeros_like(st_ref)
        _interleave(prepare(x0_ref, stage_a))

    _interleave(recurrence(stage_a, 0), prepare(xb_ref, stage_b))
    _interleave(recurrence(stage_b, hb), prepare(xa_ref, stage_a))


def _rwkv(xx, w0, wdec_pad, a0, aup_pad, gup, k_k, k_a, r_k, lnx_w, lnx_b):
    B, S, _ = xx.shape
    W, G = RWKV_WIDTH, GROUP_LANES
    hb = B // 2
    n_chunks = S // CHUNK
    head = jnp.arange(G) // RWKV_HEAD
    seg = (head[:, None] == head[None, :]).astype(BF16)
    t = jnp.arange(hb * CHUNK)
    tril = ((t[:, None] >= t[None, :]) & (t[:, None] // CHUNK == t[None, :] // CHUNK)).astype(BF16)
    tmasks = _rwkv_tile_masks()
    vec = lambda n: _const_spec((1, n))
    half = lambda dtype: pltpu.VMEM((hb * CHUNK, W), dtype)
    stage = ([half(BF16)] * len(RWKV_PREP_BF16) + [half(F32)] * len(RWKV_PREP_F32)
             + [pltpu.VMEM((hb, 8, W), F32)])
    return pl.pallas_call(
        _rwkv_kernel,
        grid=(n_chunks,),
        in_specs=[pl.BlockSpec((hb, CHUNK, RWKV_COLS), lambda j: (0, 0, 0)),
                  pl.BlockSpec((hb, CHUNK, RWKV_COLS), lambda j: (0, jnp.minimum(j + 1, n_chunks - 1), 0)),
                  pl.BlockSpec((hb, CHUNK, RWKV_COLS), lambda j: (1, j, 0)),
                  vec(W), _const_spec((LANES, W)), vec(W), _const_spec((LANES, W)),
                  _const_spec((GATE_LORA, W)), vec(W), vec(W), vec(W), vec(W), vec(W),
                  _const_spec((G, G)), _const_spec((hb * CHUNK, hb * CHUNK)),
                  _const_spec(tmasks.shape)],
        out_specs=pl.BlockSpec((B, CHUNK, W), lambda j: (0, j, 0)),
        out_shape=jax.ShapeDtypeStruct((B, S, W), BF16),
        scratch_shapes=[pltpu.VMEM((B * N_GROUPS, G, G), F32)] + stage + stage,
        compiler_params=_params("arbitrary"),
        name="rwkv7",
    )(xx, xx, xx, w0, wdec_pad, a0, aup_pad, gup, k_k, k_a, r_k, lnx_w, lnx_b, seg, tril, tmasks)


def _diff_attn_kernel(lam_init, qt_in_ref, k_ref, vt_ref, lq1_ref, lk1_ref, lq2_ref, lk2_ref, sw_ref,
                      o_ref, qt_ref, m_ref, l_ref, acc_ref, sa_ref, sb_ref, ca_ref, cb_ref):
    tq = tk = ATTN_TILE
    D, H = DIFF_VDIM, DIFF_HEADS
    hs = range(H)
    i = pl.program_id(1)

    d_row = lax.broadcasted_iota(jnp.int32, (D, tq), 0)
    for h in hs:
        qt = qt_in_ref[0, 0, h * D:(h + 1) * D, :]
        zero = jnp.zeros_like(qt)
        qt_ref[h] = jnp.concatenate([jnp.where(d_row < DIFF_QKDIM, qt, zero),
                                     jnp.where(d_row >= DIFF_QKDIM, qt, zero)], axis=1)

    m_ref[...] = jnp.full(m_ref.shape, NEG_INF, F32)
    l_ref[...] = jnp.zeros(l_ref.shape, F32)
    acc_ref[...] = jnp.zeros(acc_ref.shape, F32)

    n_kc = tk // CHUNK

    def produce(t, buf):
        s_ref, cmax_ref = buf
        off = pl.multiple_of(t * tk, tk)
        for h in hs:
            s = jnp.dot(k_ref[0, pl.ds(off, tk), h * D:(h + 1) * D], qt_ref[h],
                        preferred_element_type=F32)
            s_ref[h] = s
            cmax_ref[h] = jnp.max(s.reshape(n_kc, CHUNK, 2 * tq), axis=1)

    def consume(t, buf):
        s_ref, cmax_ref = buf
        s = [s_ref[h] for h in hs]
        m_old = [m_ref[h] for h in hs]
        m_new = [jnp.maximum(m_old[h], jnp.max(cmax_ref[h], axis=0, keepdims=True)) for h in hs]
        alpha = [jnp.exp2(m_old[h] - m_new[h]) for h in hs]
        p = [jnp.exp2(s[h] - m_new[h]) for h in hs]
        pv = [jnp.dot(vt_ref[0, t, h * D:(h + 1) * D, :], p[h].astype(BF16),
                      preferred_element_type=F32) for h in hs]
        for h in hs:
            m_ref[h] = m_new[h]
            l_ref[h] = alpha[h] * l_ref[h] + jnp.sum(p[h], axis=0, keepdims=True)
            acc_ref[h] = alpha[h] * acc_ref[h] + pv[h]

    def consume_diagonal(t, buf):
        s_ref, cmax_ref = buf
        half_q = tq // 2
        units = []
        for h in hs:
            for map_off in (0, tq):
                units.append((h, map_off, tk // 2, 0))
                units.append((h, map_off + half_q, tk, half_q))
        ids = range(len(units))

        def masks(kext, q0):
            qry_chunk = lambda shape: (lax.broadcasted_iota(jnp.int32, shape, 1) + q0) // CHUNK
            chunk_ok = lax.broadcasted_iota(jnp.int32, (kext // CHUNK, half_q), 0) <= qry_chunk((kext // CHUNK, half_q))
            key_ok = (lax.broadcasted_iota(jnp.int32, (kext, half_q), 0) // CHUNK) <= qry_chunk((kext, half_q))
            return chunk_ok, key_ok

        mask_of = {(kext, q0): masks(kext, q0) for _, _, kext, q0 in units}
        lanes = [slice(lo, lo + half_q) for _, lo, _, _ in units]
        cmax = [jnp.where(mask_of[(kext, q0)][0], cmax_ref[h][:kext // CHUNK, lanes[u]], NEG_INF)
                for u, (h, lo, kext, q0) in enumerate(units)]
        s = [jnp.where(mask_of[(kext, q0)][1], s_ref[h][:kext, lanes[u]], NEG_INF)
             for u, (h, lo, kext, q0) in enumerate(units)]
        m_old = [m_ref[units[u][0]][:, lanes[u]] for u in ids]
        m_new = [jnp.maximum(m_old[u], jnp.max(cmax[u], axis=0, keepdims=True)) for u in ids]
        alpha = [jnp.exp2(m_old[u] - m_new[u]) for u in ids]
        p = [jnp.exp2(s[u] - m_new[u]) for u in ids]
        pv = [jnp.dot(vt_ref[0, t, units[u][0] * D:(units[u][0] + 1) * D, :units[u][2]], p[u].astype(BF16),
                      preferred_element_type=F32) for u in ids]
        for u in ids:
            h = units[u][0]
            m_ref[h, :, lanes[u]] = m_new[u]
            l_ref[h, :, lanes[u]] = alpha[u] * l_ref[h, :, lanes[u]] + jnp.sum(p[u], axis=0, keepdims=True)
            acc_ref[h, :, lanes[u]] = alpha[u] * acc_ref[h, :, lanes[u]] + pv[u]

    def finish(buf):
        consume_diagonal(i, buf)
        lam = (jnp.exp(jnp.sum(lq1_ref[...] * lk1_ref[...], axis=-1, keepdims=True))
               - jnp.exp(jnp.sum(lq2_ref[...] * lk2_ref[...], axis=-1, keepdims=True)) + lam_init)
        outs = []
        for h in hs:
            o2 = acc_ref[h] / l_ref[h]
            o = o2[:, :tq] - lam * o2[:, tq:]
            o = o * lax.rsqrt(jnp.mean(o * o, axis=0, keepdims=True) + RMS_EPS) * sw_ref[...]
            outs.append((o * (1.0 - lam_init)).T)
        o_ref[0] = jnp.concatenate(outs, axis=1).astype(o_ref.dtype)

    buf_a, buf_b = (sa_ref, ca_ref), (sb_ref, cb_ref)
    produce(0, buf_a)

    def pair(pi, carry):
        t = 2 * pi
        produce(t + 1, buf_b)
        consume(t, buf_a)
        produce(t + 2, buf_a)
        consume(t + 1, buf_b)
        return carry

    lax.fori_loop(0, i // 2, pair, 0)

    @pl.when(i % 2 == 1)
    def _():
        produce(i, buf_b)
        consume(i - 1, buf_a)
        finish(buf_b)

    @pl.when(i % 2 == 0)
    def _():
        finish(buf_a)


def _diff_attn(q_t, k, v_t, lam_q1, lam_k1, lam_q2, lam_k2, subln_col, lam_init):
    B, S, _ = k.shape
    H, D, tq = DIFF_HEADS, DIFF_VDIM, ATTN_TILE
    lamv = _const_spec((1, DIFF_QKDIM))
    scores = pltpu.VMEM((H, tq, 2 * tq), F32)
    chunk_max = pltpu.VMEM((H, tq // CHUNK, 2 * tq), F32)
    return pl.pallas_call(
        functools.partial(_diff_attn_kernel, lam_init),
        grid=(B, S // tq),
        in_specs=[pl.BlockSpec((1, 1, DIFF_WIDTH, tq), lambda b, i: (b, i, 0, 0)),
                  pl.BlockSpec((1, S, DIFF_WIDTH), lambda b, i: (b, 0, 0)),
                  pl.BlockSpec((1, S // tq, DIFF_WIDTH, tq), lambda b, i: (b, 0, 0, 0)),
                  lamv, lamv, lamv, lamv, _const_spec((D, 1))],
        out_specs=pl.BlockSpec((1, tq, DIFF_WIDTH), lambda b, i: (b, i, 0)),
        out_shape=jax.ShapeDtypeStruct((B, S, DIFF_WIDTH), BF16),
        scratch_shapes=[pltpu.VMEM((H, D, 2 * tq), BF16),
                        pltpu.VMEM((H, 1, 2 * tq), F32), pltpu.VMEM((H, 1, 2 * tq), F32),
                        pltpu.VMEM((H, D, 2 * tq), F32), scores, scores, chunk_max, chunk_max],
        compiler_params=_params("parallel", "arbitrary"),
        name="diff_attn",
    )(q_t, k, v_t, lam_q1, lam_k1, lam_q2, lam_k2, subln_col)


def _mem_kv_kernel(m_ref, nw_ref, wk_ref, wv_ref, k_ref, v_ref):
    mn = _rms(m_ref[...], nw_ref[...]).astype(BF16)
    k_ref[...] = jnp.dot(mn, wk_ref[...].astype(BF16), preferred_element_type=F32).astype(BF16)
    v_ref[...] = jnp.dot(mn, wv_ref[...].astype(BF16), preferred_element_type=F32).astype(BF16)


def _mem_kv(mem2, norm_w, w_mk, w_mv):
    n = mem2.shape[0]
    wspec = _const_spec((D_MODEL, D_MODEL))
    rows = pl.BlockSpec((ROW_TILE, D_MODEL), lambda i: (i, 0))
    return pl.pallas_call(
        _mem_kv_kernel,
        grid=(n // ROW_TILE,),
        in_specs=[rows, _const_spec((1, D_MODEL)), wspec, wspec],
        out_specs=[rows, rows],
        out_shape=[jax.ShapeDtypeStruct((n, D_MODEL), BF16)] * 2,
        compiler_params=_params("parallel"),
        name="mem_kv",
    )(mem2, norm_w, w_mk, w_mv)


def _mix_mem_kernel(x_ref, yr_ref, yd_ref, wo_ref, nw_ref, wq_ref, k_ref, v_ref, wmo_ref, o_ref):
    h = (x_ref[0]
         + jnp.dot(yr_ref[0], wo_ref[:RWKV_WIDTH, :].astype(BF16), preferred_element_type=F32)
         + jnp.dot(yd_ref[0], wo_ref[RWKV_WIDTH:, :].astype(BF16), preferred_element_type=F32))
    hn = _rms(h, nw_ref[...]).astype(BF16)
    q = jnp.dot(hn, wq_ref[...].astype(BF16), preferred_element_type=F32) * (MEM_HEAD_DIM ** -0.5)
    q = q.astype(BF16)
    outs = []
    for hd in range(MEM_HEADS):
        sl = slice(hd * MEM_HEAD_DIM, (hd + 1) * MEM_HEAD_DIM)
        s = lax.dot_general(q[:, sl], k_ref[0, :, sl], (((1,), (1,)), ((), ())),
                            preferred_element_type=F32)
        pr = jnp.exp(s - jnp.max(s, axis=-1, keepdims=True))
        den = jnp.sum(pr, axis=-1, keepdims=True)
        outs.append(jnp.dot(pr.astype(BF16), v_ref[0, :, sl], preferred_element_type=F32) / den)
    o = jnp.concatenate(outs, axis=1).astype(BF16)
    o_ref[0] = h + jnp.dot(o, wmo_ref[...].astype(BF16), preferred_element_type=F32)


def _mix_mem(x, y_rwkv, y_diff, w_out, norm_w, w_mq, k_mem, v_mem, w_mo):
    B, S, D = x.shape
    M = k_mem.shape[1]
    wspec = _const_spec((D, D))
    rows = lambda w: pl.BlockSpec((1, ROW_TILE, w), lambda b, i: (b, i, 0))
    memspec = pl.BlockSpec((1, M, D), lambda b, i: (b, 0, 0))
    return pl.pallas_call(
        _mix_mem_kernel,
        grid=(B, S // ROW_TILE),
        in_specs=[rows(D), rows(RWKV_WIDTH), rows(DIFF_WIDTH), wspec, _const_spec((1, D)), wspec,
                  memspec, memspec, wspec],
        out_specs=rows(D),
        out_shape=jax.ShapeDtypeStruct((B, S, D), F32),
        compiler_params=_params("parallel", "parallel"),
        name="mix_mem",
    )(x, y_rwkv, y_diff, w_out, norm_w, w_mq, k_mem, v_mem, w_mo)


def _mlp_kernel(final_norm, h_ref, nw_ref, wu_ref, wd_ref, fw_ref, o_ref):
    h = h_ref[...]
    hn = _rms(h, nw_ref[...]).astype(BF16)
    acc = h
    for c in range(D_FF // FF_TILE):
        sl = slice(c * FF_TILE, (c + 1) * FF_TILE)
        u = jnp.maximum(jnp.dot(hn, wu_ref[:, sl].astype(BF16), preferred_element_type=F32), 0.0)
        acc = acc + jnp.dot((u * u).astype(BF16), wd_ref[sl, :].astype(BF16), preferred_element_type=F32)
    o_ref[...] = _rms(acc, fw_ref[...]) if final_norm else acc


def _mlp(h2, norm_w, w_up, w_down, final_w, final_norm):
    n, D = h2.shape
    rows = pl.BlockSpec((MLP_ROW_TILE, D), lambda i: (i, 0))
    return pl.pallas_call(
        functools.partial(_mlp_kernel, final_norm),
        grid=(n // MLP_ROW_TILE,),
        in_specs=[rows, _const_spec((1, D)), _const_spec((D, D_FF)), _const_spec((D_FF, D)),
                  _const_spec((1, D))],
        out_specs=rows,
        out_shape=jax.ShapeDtypeStruct((n, D), F32),
        compiler_params=_params("parallel"),
        name="mlp",
    )(h2, norm_w, w_up, w_down, final_w)


def kernel(x, mem, norm_mix_w, w_in, mu_shift, w_decay0, w_decay_up, a0, a_up, g_up, k_k, k_a, r_k,
           lnx_w, lnx_b, lam_q1, lam_k1, lam_q2, lam_k2, subln_w, w_out, norm_mem_w, norm_src_w,
           w_mq, w_mk, w_mv, w_mo, norm_mlp_w, w_up, w_down, norm_final_w):
    B, S, D = x.shape
    depth = norm_mix_w.shape[0]
    row = lambda t: t.reshape(1, -1)
    h = x
    for l in range(depth):
        lam_init = 0.8 - 0.6 * math.exp(-0.3 * l)
        zeros_lora = jnp.zeros((DECAY_LORA, RWKV_WIDTH), BF16)
        wdec_pad = jnp.concatenate([w_decay_up[l].astype(BF16), zeros_lora], axis=0)
        aup_pad = jnp.concatenate([zeros_lora, a_up[l].astype(BF16)], axis=0)

        n_cols = RWKV_COLS + 2 * DIFF_WIDTH
        w_qv_t = jnp.concatenate([w_in[l][:, RWKV_COLS:K_COL0], w_in[l][:, n_cols:]], axis=1).T
        p_rwkv, k_att, q_t, v_t = _in_proj(h.reshape(B * S, D), row(norm_mix_w[l]),
                                           w_in[l][:, :n_cols].astype(BF16), w_qv_t.astype(BF16),
                                           row(mu_shift[l]), B)
        y_rwkv = _rwkv(p_rwkv.reshape(B, S, RWKV_COLS), row(w_decay0[l]),
                       wdec_pad, row(a0[l]), aup_pad, g_up[l].astype(BF16), row(k_k[l]),
                       row(k_a[l]), row(r_k[l]), row(lnx_w[l]), row(lnx_b[l]))
        y_diff = _diff_attn(q_t, k_att.reshape(B, S, DIFF_WIDTH), v_t, row(lam_q1[l]), row(lam_k1[l]),
                            row(lam_q2[l]), row(lam_k2[l]), subln_w[l].reshape(-1, 1), lam_init)
        k_mem, v_mem = _mem_kv(mem.reshape(-1, D), row(norm_src_w[l]), w_mk[l], w_mv[l])
        M = mem.shape[1]
        h = _mix_mem(h, y_rwkv, y_diff, w_out[l], row(norm_mem_w[l]), w_mq[l],
                     k_mem.reshape(B, M, D), v_mem.reshape(B, M, D), w_mo[l])
        h = _mlp(h.reshape(B * S, D), row(norm_mlp_w[l]), w_up[l], w_down[l], row(norm_final_w),
                 l == depth - 1)
        h = h.reshape(B, S, D)
    return h
```

```python
import functools
import math

import jax
import jax.numpy as jnp
from jax import lax
from jax.experimental import pallas as pl
from jax.experimental.pallas import tpu as pltpu

F32 = jnp.float32
BF16 = jnp.bfloat16

D_MODEL = 1024
CHUNK = 64
RWKV_HEAD = 64
RWKV_WIDTH = 512
RWKV_HEADS = 8
DECAY_LORA = 64
AAA_LORA = 64
GATE_LORA = 128
RWKV_COLS = 3 * RWKV_WIDTH + DECAY_LORA + AAA_LORA + GATE_LORA
GN_EPS = 64e-5
DIFF_WIDTH = 512
DIFF_HEADS = 4
DIFF_VDIM = 128
DIFF_QKDIM = 64
DIFF_COLS = 3 * DIFF_WIDTH
D_IN_TOTAL = RWKV_COLS + DIFF_COLS
MEM_HEADS = 4
MEM_HEAD_DIM = 256
D_FF = 4 * D_MODEL
RMS_EPS = 1e-5
NEG_INF = -1e30
LOG2_E = 1.4426950408889634

VMEM_LIMIT_BYTES = 56 * 1024 * 1024

ROW_TILE = 1024
MLP_ROW_TILE = 512
ATTN_TILE = 512
FF_TILE = 1024


def _mm(a, b):
    return jnp.dot(a.astype(BF16), b.astype(BF16), preferred_element_type=F32)


def _mm_nt(a, b):
    return lax.dot_general(a.astype(BF16), b.astype(BF16), (((1,), (1,)), ((), ())),
                           preferred_element_type=F32)


def _mm_tn(a, b):
    return lax.dot_general(a.astype(BF16), b.astype(BF16), (((0,), (0,)), ((), ())),
                           preferred_element_type=F32)


def _rms(x, w, eps=RMS_EPS):
    return x * lax.rsqrt(jnp.mean(x * x, axis=-1, keepdims=True) + eps) * w


def _params(*sem, flags=None):
    return pltpu.CompilerParams(dimension_semantics=sem, vmem_limit_bytes=VMEM_LIMIT_BYTES,
                                flags=flags)


def _interleave(*gens):
    live = list(gens)
    while live:
        for gen in list(live):
            if next(gen, StopIteration) is StopIteration:
                live.remove(gen)


def _const_spec(shape):
    nd = len(shape)
    return pl.BlockSpec(shape, lambda *_: (0,) * nd, pipeline_mode=pl.Buffered(1))


K_COL0 = RWKV_COLS + DIFF_WIDTH


def _in_proj_kernel(tiles_per_seq, x_ref, nw_ref, w_ref, wqvt_ref, mu_ref, pr_ref, k_ref, qt_ref, vt_ref,
                    carry_ref):
    i = pl.program_id(0)
    rows = x_ref.shape[0]

    @pl.when(i == 0)
    def _():
        carry_ref[...] = jnp.zeros_like(carry_ref)

    xn = _rms(x_ref[...], nw_ref[...]).astype(BF16)
    p = jnp.dot(xn, w_ref[:, :RWKV_COLS], preferred_element_type=F32)
    last_prev = jnp.where(i % tiles_per_seq == 0, 0.0, carry_ref[...])
    first_row = lax.broadcasted_iota(jnp.int32, p.shape, 0) == 0
    p_prev = jnp.where(first_row, last_prev, pltpu.roll(p, 1, 0))
    carry_ref[...] = p[rows - 1:rows, :]
    pr_ref[...] = p + (p_prev - p) * mu_ref[...]
    k_ref[...] = jnp.dot(xn, w_ref[:, K_COL0:], preferred_element_type=F32).astype(BF16)
    qv_t = lax.dot_general(wqvt_ref[...], xn, (((1,), (1,)), ((), ())), preferred_element_type=F32)
    qt_ref[0, 0] = (qv_t[:DIFF_WIDTH] * (DIFF_QKDIM ** -0.5 * LOG2_E)).astype(BF16)
    vt_ref[0, 0] = qv_t[DIFF_WIDTH:].astype(BF16)


def _in_proj(x2, norm_w, w_in_bf, w_qvt_bf, mu, batch):
    n = x2.shape[0]
    tiles_per_seq = n // batch // ATTN_TILE
    n_cols = RWKV_COLS + 2 * DIFF_WIDTH
    t_spec = pl.BlockSpec((1, 1, DIFF_WIDTH, ATTN_TILE),
                          lambda i: (i // tiles_per_seq, i % tiles_per_seq, 0, 0))
    t_shape = jax.ShapeDtypeStruct((batch, tiles_per_seq, DIFF_WIDTH, ATTN_TILE), BF16)
    return pl.pallas_call(
        functools.partial(_in_proj_kernel, tiles_per_seq),
        grid=(n // ATTN_TILE,),
        in_specs=[pl.BlockSpec((ATTN_TILE, D_MODEL), lambda i: (i, 0)),
                  _const_spec((1, D_MODEL)),
                  _const_spec((D_MODEL, n_cols)),
                  _const_spec((2 * DIFF_WIDTH, D_MODEL)),
                  _const_spec((1, RWKV_COLS))],
        out_specs=[pl.BlockSpec((ATTN_TILE, RWKV_COLS), lambda i: (i, 0)),
                   pl.BlockSpec((ATTN_TILE, DIFF_WIDTH), lambda i: (i, 0)),
                   t_spec, t_spec],
        out_shape=[jax.ShapeDtypeStruct((n, RWKV_COLS), F32),
                   jax.ShapeDtypeStruct((n, DIFF_WIDTH), BF16),
                   t_shape, t_shape],
        scratch_shapes=[pltpu.VMEM((1, RWKV_COLS), F32)],
        compiler_params=_params("arbitrary"),
        name="in_proj",
    )(x2, norm_w, w_in_bf, w_qvt_bf, mu)


GROUP_LANES = 256
HEADS_PER_GROUP = GROUP_LANES // RWKV_HEAD
N_GROUPS = RWKV_WIDTH // GROUP_LANES
LANES = 128
RWKV_PREP_BF16 = ("kd", "rd", "bi", "ki", "bw", "kw", "v")
RWKV_PREP_F32 = ("bonus", "gate")


def _rwkv_tile_masks():
    r = jnp.arange(CHUNK)[:, None]
    c = (jnp.arange(GROUP_LANES) % CHUNK)[None, :]
    ms = [r > c, r >= c, r == c, (r > c) & ((r // 8) == (c // 8))]
    size = 8
    while size < CHUNK:
        ms.append(((r // (2 * size)) == (c // (2 * size))) & ((r // size) > (c // size)))
        size *= 2
    return jnp.stack(ms).astype(F32)


def _rwkv_kernel(x0_ref, xa_ref, xb_ref, w0_ref, wdec_ref, a0_ref, aup_ref, gup_ref, kk_ref, ka_ref,
                 rk_ref, lnw_ref, lnb_ref, seg_ref, tril_ref, tmask_ref, o_ref, st_ref, *prep_refs):
    T, W, G = CHUNK, RWKV_WIDTH, GROUP_LANES
    hb = xa_ref.shape[0]
    n_names = len(RWKV_PREP_BF16) + len(RWKV_PREP_F32) + 1
    names = RWKV_PREP_BF16 + RWKV_PREP_F32 + ("e_last",)
    stage_a = dict(zip(names, prep_refs[:n_names]))
    stage_b = dict(zip(names, prep_refs[n_names:]))

    seg = seg_ref[...]

    def head_sum(x):
        xb = x.astype(BF16)
        return jnp.concatenate(
            [jnp.dot(xb[:, g * G:(g + 1) * G], seg, preferred_element_type=F32)
             for g in range(N_GROUPS)], axis=1)

    def prepare(x_ref, stage):
        xx = jnp.concatenate([x_ref[b] for b in range(hb)], axis=0)
        r = xx[:, 0:W]
        k = xx[:, W:2 * W]
        v = xx[:, 2 * W:3 * W]
        lora_da = xx[:, 3 * W:3 * W + DECAY_LORA + AAA_LORA]
        gd = xx[:, 3 * W + DECAY_LORA + AAA_LORA:]

        z = w0_ref[...] + _mm(jnp.tanh(lora_da), wdec_ref[...])
        yield
        softplus_neg = jnp.maximum(-z, 0.0) + jnp.log(1.0 + jnp.exp(-jnp.abs(z)))
        yield
        log_decay = -jnp.exp(-softplus_neg - 0.5)
        yield
        a = jax.nn.sigmoid(a0_ref[...] + _mm(lora_da, aup_ref[...]))
        yield
        stage["gate"][...] = _mm(jax.nn.sigmoid(gd), gup_ref[...])
        yield

        kk = k * kk_ref[...]
        kk = kk * lax.rsqrt(jnp.maximum(head_sum(kk * kk), 1e-24))
        yield
        k_mod = k * (1.0 + (a - 1.0) * ka_ref[...])
        b_vec = kk * a
        yield

        ld_hi = log_decay.astype(BF16)
        ld_lo = (log_decay - ld_hi.astype(F32)).astype(BF16)
        tril = tril_ref[...]
        cl = (jnp.dot(tril, ld_hi, preferred_element_type=F32)
              + jnp.dot(tril, ld_lo, preferred_element_type=F32))
        yield
        e_incl = jnp.exp(cl)
        yield
        e_inv = jnp.exp(-cl)
        yield
        bi = b_vec * e_inv
        stage["bi"][...] = bi.astype(BF16)
        yield
        ki = k_mod * e_inv
        stage["ki"][...] = ki.astype(BF16)
        yield
        stage["kd"][...] = (kk * jnp.exp(cl - log_decay)).astype(BF16)
        yield
        stage["rd"][...] = (r * e_incl).astype(BF16)
        stage["v"][...] = v.astype(BF16)
        yield
        stage["bonus"][...] = head_sum(r * k_mod * rk_ref[...]) * v
        yield
        for b in range(hb):
            rows = slice(b * T, (b + 1) * T)
            w_total = e_incl[b * T + T - 1:b * T + T, :]
            stage["bw"][rows, :] = (bi[rows] * w_total).astype(BF16)
            stage["kw"][rows, :] = (ki[rows] * w_total).astype(BF16)
            stage["e_last"][b] = jnp.broadcast_to(w_total, (8, W))
            if b % 2 == 1:
                yield

    bdm32 = seg.astype(F32)
    m_strict, m_incl, m_eye, m_blk8 = (tmask_ref[i] for i in range(4))
    m_levels = [tmask_ref[i] for i in range(4, tmask_ref.shape[0])]
    m_both = jnp.concatenate([m_strict, m_incl], axis=0)

    lane_in_vreg = lax.broadcasted_iota(jnp.int32, (T, G), 1) % LANES
    half_masks = [jnp.where((lane_in_vreg // RWKV_HEAD) == j, 1.0, 0.0).astype(BF16)
                  for j in range(LANES // RWKV_HEAD)]
    zero_vreg_cols = jnp.zeros((T, LANES), BF16)

    def bd(y):
        yb = y.astype(BF16)
        kept = [yb * m for m in half_masks]
        rows = []
        for h in range(HEADS_PER_GROUP):
            col, j = divmod(h * RWKV_HEAD, LANES)
            j //= RWKV_HEAD
            rows.append(jnp.concatenate(
                [kept[j][:, c * LANES:(c + 1) * LANES] if c == col else zero_vreg_cols
                 for c in range(G // LANES)], axis=1))
        return jnp.concatenate(rows, axis=0)

    def recurrence(stage, b0):
        items = [(b, g) for b in range(hb) for g in range(N_GROUPS)]
        n_it = range(len(items))
        tile = lambda name, b, g: stage[name][b * T:(b + 1) * T, g * G:(g + 1) * G]
        kd = [tile("kd", b, g) for b, g in items]
        bi = [tile("bi", b, g) for b, g in items]
        ki = [tile("ki", b, g) for b, g in items]
        vv = [tile("v", b, g) for b, g in items]

        w_col = {}
        for b in range(hb):
            col = jnp.broadcast_to(stage["e_last"][b][0:1, :], (LANES, W)).T
            for g in range(N_GROUPS):
                w_col[(b, g)] = jnp.concatenate([col[g * G:(g + 1) * G]] * (G // LANES), axis=1)

        lhs = [jnp.concatenate([kd[i], tile("rd", *items[i])], axis=0) for i in n_it]
        a1 = [_mm(lhs[i], bd(bi[i]).astype(F32).T) for i in n_it]
        yield
        a2 = [_mm(lhs[i], bd(ki[i]).astype(F32).T) for i in n_it]
        yield
        a_k = [a2[i] * m_both for i in n_it]
        a_rb = [a1[i][T:] * m_incl for i in n_it]
        yield

        n0 = [a1[i][:T] * m_blk8 for i in n_it]
        n2 = [_mm(n0[i], bd(n0[i])) for i in n_it]
        yield
        n4 = [_mm(n2[i], bd(n2[i])) for i in n_it]
        yield
        x = [_mm(m_eye - n0[i], bd(m_eye + n2[i])) for i in n_it]
        yield
        x = [_mm(x[i], bd(m_eye + n4[i])) for i in n_it]
        yield
        for m_off in m_levels:
            t1 = [_mm(x[i], bd(a1[i][:T] * m_off)) for i in n_it]
            yield
            x = [x[i] - _mm(t1[i], bd(x[i])) for i in n_it]
            yield

        st = [st_ref[(b0 + b) * N_GROUPS + g] for b, g in items]
        ks = [jnp.dot(lhs[i], st[i].astype(BF16), preferred_element_type=F32) for i in n_it]
        yield
        sv = [ks[i] + _mm(a_k[i], bd(vv[i])) for i in n_it]
        yield
        u = [-_mm(x[i], bd(sv[i][:T])) for i in n_it]
        yield
        y_tiles = [sv[i][T:] + _mm(a_rb[i], bd(u[i])) for i in n_it]
        yield
        for i, (b, g) in enumerate(items):
            wk = jnp.concatenate([tile("bw", b, g), tile("kw", b, g)], axis=0)
            grow = _mm_tn(wk, jnp.concatenate([u[i].astype(BF16), vv[i]], axis=0)) * bdm32
            st_ref[(b0 + b) * N_GROUPS + g] = st[i] * w_col[(b, g)] + grow
            if i % 4 == 3:
                yield

        y = jnp.concatenate(
            [jnp.concatenate(y_tiles[b * N_GROUPS:(b + 1) * N_GROUPS], axis=1) for b in range(hb)],
            axis=0)
        inv_n = 1.0 / RWKV_HEAD
        mean = head_sum(y) * inv_n
        d = y - mean
        var = head_sum(d * d) * inv_n
        yield
        yn = d * lax.rsqrt(var + GN_EPS) * lnw_ref[...] + lnb_ref[...]
        out = ((yn + stage["bonus"][...]) * stage["gate"][...]).astype(o_ref.dtype)
        for b in range(hb):
            o_ref[b0 + b] = out[b * T:(b + 1) * T]

    @pl.when(pl.program_id(0) == 0)
    def _():
        st_ref[...] = jnp.zeros_like(st_ref)
        _interleave(prepare(x0_ref, stage_a))

    _interleave(recurrence(stage_a, 0), prepare(xb_ref, stage_b))
    _interleave(recurrence(stage_b, hb), prepare(xa_ref, stage_a))


def _rwkv(xx, w0, wdec_pad, a0, aup_pad, gup, k_k, k_a, r_k, lnx_w, lnx_b):
    B, S, _ = xx.shape
    W, G = RWKV_WIDTH, GROUP_LANES
    hb = B // 2
    n_chunks = S // CHUNK
    head = jnp.arange(G) // RWKV_HEAD
    seg = (head[:, None] == head[None, :]).astype(BF16)
    t = jnp.arange(hb * CHUNK)
    tril = ((t[:, None] >= t[None, :]) & (t[:, None] // CHUNK == t[None, :] // CHUNK)).astype(BF16)
    tmasks = _rwkv_tile_masks()
    vec = lambda n: _const_spec((1, n))
    half = lambda dtype: pltpu.VMEM((hb * CHUNK, W), dtype)
    stage = ([half(BF16)] * len(RWKV_PREP_BF16) + [half(F32)] * len(RWKV_PREP_F32)
             + [pltpu.VMEM((hb, 8, W), F32)])
    return pl.pallas_call(
        _rwkv_kernel,
        grid=(n_chunks,),
        in_specs=[pl.BlockSpec((hb, CHUNK, RWKV_COLS), lambda j: (0, 0, 0)),
                  pl.BlockSpec((hb, CHUNK, RWKV_COLS), lambda j: (0, jnp.minimum(j + 1, n_chunks - 1), 0)),
                  pl.BlockSpec((hb, CHUNK, RWKV_COLS), lambda j: (1, j, 0)),
                  vec(W), _const_spec((LANES, W)), vec(W), _const_spec((LANES, W)),
                  _const_spec((GATE_LORA, W)), vec(W), vec(W), vec(W), vec(W), vec(W),
                  _const_spec((G, G)), _const_spec((hb * CHUNK, hb * CHUNK)),
                  _const_spec(tmasks.shape)],
        out_specs=pl.BlockSpec((B, CHUNK, W), lambda j: (0, j, 0)),
        out_shape=jax.ShapeDtypeStruct((B, S, W), BF16),
        scratch_shapes=[pltpu.VMEM((B * N_GROUPS, G, G), F32)] + stage + stage,
        compiler_params=_params("arbitrary"),
        name="rwkv7",
    )(xx, xx, xx, w0, wdec_pad, a0, aup_pad, gup, k_k, k_a, r_k, lnx_w, lnx_b, seg, tril, tmasks)


def _diff_attn_kernel(lam_init, qt_in_ref, k_ref, vt_ref, lq1_ref, lk1_ref, lq2_ref, lk2_ref, sw_ref,
                      o_ref, qt_ref, m_ref, l_ref, acc_ref, sa_ref, sb_ref, ca_ref, cb_ref):
    tq = tk = ATTN_TILE
    D, H = DIFF_VDIM, DIFF_HEADS
    hs = range(H)
    i = pl.program_id(1)

    d_row = lax.broadcasted_iota(jnp.int32, (D, tq), 0)
    for h in hs:
        qt = qt_in_ref[0, 0, h * D:(h + 1) * D, :]
        zero = jnp.zeros_like(qt)
        qt_ref[h] = jnp.concatenate([jnp.where(d_row < DIFF_QKDIM, qt, zero),
                                     jnp.where(d_row >= DIFF_QKDIM, qt, zero)], axis=1)

    m_ref[...] = jnp.full(m_ref.shape, NEG_INF, F32)
    l_ref[...] = jnp.zeros(l_ref.shape, F32)
    acc_ref[...] = jnp.zeros(acc_ref.shape, F32)

    n_kc = tk // CHUNK

    def produce(t, buf):
        s_ref, cmax_ref = buf
        off = pl.multiple_of(t * tk, tk)
        for h in hs:
            s = jnp.dot(k_ref[0, pl.ds(off, tk), h * D:(h + 1) * D], qt_ref[h],
                        preferred_element_type=F32)
            s_ref[h] = s
            cmax_ref[h] = jnp.max(s.reshape(n_kc, CHUNK, 2 * tq), axis=1)

    def produce_diagonal(t, buf):
        s_ref, cmax_ref = buf
        off = pl.multiple_of(t * tk, tk)
        hk, hq = tk // 2, tq // 2
        for h in hs:
            top = jnp.dot(k_ref[0, pl.ds(off, hk), h * D:(h + 1) * D], qt_ref[h],
                          preferred_element_type=F32)
            s_ref[h, :hk, :] = top
            cmax_ref[h, :n_kc // 2, :] = jnp.max(top.reshape(n_kc // 2, CHUNK, 2 * tq), axis=1)
            for map_off in (0, tq):
                lanes = slice(map_off + hq, map_off + tq)
                bot = jnp.dot(k_ref[0, pl.ds(off + hk, hk), h * D:(h + 1) * D], qt_ref[h, :, lanes],
                              preferred_element_type=F32)
                s_ref[h, hk:, lanes] = bot
                cmax_ref[h, n_kc // 2:, lanes] = jnp.max(bot.reshape(n_kc // 2, CHUNK, hq), axis=1)

    def consume(t, buf):
        s_ref, cmax_ref = buf
        s = [s_ref[h] for h in hs]
        m_old = [m_ref[h] for h in hs]
        m_new = [jnp.maximum(m_old[h], jnp.max(cmax_ref[h], axis=0, keepdims=True)) for h in hs]
        alpha = [jnp.exp2(m_old[h] - m_new[h]) for h in hs]
        p = [jnp.exp2(s[h] - m_new[h]) for h in hs]
        pv = [jnp.dot(vt_ref[0, t, h * D:(h + 1) * D, :], p[h].astype(BF16),
                      preferred_element_type=F32) for h in hs]
        for h in hs:
            m_ref[h] = m_new[h]
            l_ref[h] = alpha[h] * l_ref[h] + jnp.sum(p[h], axis=0, keepdims=True)
            acc_ref[h] = alpha[h] * acc_ref[h] + pv[h]

    def consume_diagonal(t, buf):
        s_ref, cmax_ref = buf
        half_q = tq // 2
        units = []
        for h in hs:
            for map_off in (0, tq):
                units.append((h, map_off, tk // 2, 0))
                units.append((h, map_off + half_q, tk, half_q))
        ids = range(len(units))

        def masks(kext, q0):
            qry_chunk = lambda shape: (lax.broadcasted_iota(jnp.int32, shape, 1) + q0) // CHUNK
            chunk_ok = lax.broadcasted_iota(jnp.int32, (kext // CHUNK, half_q), 0) <= qry_chunk((kext // CHUNK, half_q))
            key_ok = (lax.broadcasted_iota(jnp.int32, (kext, half_q), 0) // CHUNK) <= qry_chunk((kext, half_q))
            return chunk_ok, key_ok

        mask_of = {(kext, q0): masks(kext, q0) for _, _, kext, q0 in units}
        lanes = [slice(lo, lo + half_q) for _, lo, _, _ in units]
        cmax = [jnp.where(mask_of[(kext, q0)][0], cmax_ref[h][:kext // CHUNK, lanes[u]], NEG_INF)
                for u, (h, lo, kext, q0) in enumerate(units)]
        s = [jnp.where(mask_of[(kext, q0)][1], s_ref[h][:kext, lanes[u]], NEG_INF)
             for u, (h, lo, kext, q0) in enumerate(units)]
        m_old = [m_ref[units[u][0]][:, lanes[u]] for u in ids]
        m_new = [jnp.maximum(m_old[u], jnp.max(cmax[u], axis=0, keepdims=True)) for u in ids]
        alpha = [jnp.exp2(m_old[u] - m_new[u]) for u in ids]
        p = [jnp.exp2(s[u] - m_new[u]) for u in ids]
        pv = [jnp.dot(vt_ref[0, t, units[u][0] * D:(units[u][0] + 1) * D, :units[u][2]], p[u].astype(BF16),
                      preferred_element_type=F32) for u in ids]
        for u in ids:
            h = units[u][0]
            m_ref[h, :, lanes[u]] = m_new[u]
            l_ref[h, :, lanes[u]] = alpha[u] * l_ref[h, :, lanes[u]] + jnp.sum(p[u], axis=0, keepdims=True)
            acc_ref[h, :, lanes[u]] = alpha[u] * acc_ref[h, :, lanes[u]] + pv[u]

    def finish(buf):
        consume_diagonal(i, buf)
        lam = (jnp.exp(jnp.sum(lq1_ref[...] * lk1_ref[...], axis=-1, keepdims=True))
               - jnp.exp(jnp.sum(lq2_ref[...] * lk2_ref[...], axis=-1, keepdims=True)) + lam_init)
        outs = []
        for h in hs:
            o2 = acc_ref[h] / l_ref[h]
            o = o2[:, :tq] - lam * o2[:, tq:]
            o = o * lax.rsqrt(jnp.mean(o * o, axis=0, keepdims=True) + RMS_EPS) * sw_ref[...]
            outs.append((o * (1.0 - lam_init)).T)
        o_ref[0] = jnp.concatenate(outs, axis=1).astype(o_ref.dtype)

    buf_a, buf_b = (sa_ref, ca_ref), (sb_ref, cb_ref)
    produce(0, buf_a)

    def pair(pi, carry):
        t = 2 * pi
        produce(t + 1, buf_b)
        consume(t, buf_a)
        produce(t + 2, buf_a)
        consume(t + 1, buf_b)
        return carry

    lax.fori_loop(0, i // 2, pair, 0)

    @pl.when(i % 2 == 1)
    def _():
        produce_diagonal(i, buf_b)
        consume(i - 1, buf_a)
        finish(buf_b)

    @pl.when(i % 2 == 0)
    def _():
        finish(buf_a)


def _diff_attn(q_t, k, v_t, lam_q1, lam_k1, lam_q2, lam_k2, subln_col, lam_init):
    B, S, _ = k.shape
    H, D, tq = DIFF_HEADS, DIFF_VDIM, ATTN_TILE
    lamv = _const_spec((1, DIFF_QKDIM))
    scores = pltpu.VMEM((H, tq, 2 * tq), F32)
    chunk_max = pltpu.VMEM((H, tq // CHUNK, 2 * tq), F32)
    return pl.pallas_call(
        functools.partial(_diff_attn_kernel, lam_init),
        grid=(B, S // tq),
        in_specs=[pl.BlockSpec((1, 1, DIFF_WIDTH, tq), lambda b, i: (b, i, 0, 0)),
                  pl.BlockSpec((1, S, DIFF_WIDTH), lambda b, i: (b, 0, 0)),
                  pl.BlockSpec((1, S // tq, DIFF_WIDTH, tq), lambda b, i: (b, 0, 0, 0)),
                  lamv, lamv, lamv, lamv, _const_spec((D, 1))],
        out_specs=pl.BlockSpec((1, tq, DIFF_WIDTH), lambda b, i: (b, i, 0)),
        out_shape=jax.ShapeDtypeStruct((B, S, DIFF_WIDTH), BF16),
        scratch_shapes=[pltpu.VMEM((H, D, 2 * tq), BF16),
                        pltpu.VMEM((H, 1, 2 * tq), F32), pltpu.VMEM((H, 1, 2 * tq), F32),
                        pltpu.VMEM((H, D, 2 * tq), F32), scores, scores, chunk_max, chunk_max],
        compiler_params=_params("parallel", "arbitrary"),
        name="diff_attn",
    )(q_t, k, v_t, lam_q1, lam_k1, lam_q2, lam_k2, subln_col)


def _mem_kv_kernel(m_ref, nw_ref, wk_ref, wv_ref, k_ref, v_ref):
    mn = _rms(m_ref[...], nw_ref[...]).astype(BF16)
    k_ref[...] = jnp.dot(mn, wk_ref[...].astype(BF16), preferred_element_type=F32).astype(BF16)
    v_ref[...] = jnp.dot(mn, wv_ref[...].astype(BF16), preferred_element_type=F32).astype(BF16)


def _mem_kv(mem2, norm_w, w_mk, w_mv):
    n = mem2.shape[0]
    wspec = _const_spec((D_MODEL, D_MODEL))
    rows = pl.BlockSpec((ROW_TILE, D_MODEL), lambda i: (i, 0))
    return pl.pallas_call(
        _mem_kv_kernel,
        grid=(n // ROW_TILE,),
        in_specs=[rows, _const_spec((1, D_MODEL)), wspec, wspec],
        out_specs=[rows, rows],
        out_shape=[jax.ShapeDtypeStruct((n, D_MODEL), BF16)] * 2,
        compiler_params=_params("parallel"),
        name="mem_kv",
    )(mem2, norm_w, w_mk, w_mv)


def _mix_mem_kernel(x_ref, yr_ref, yd_ref, wo_ref, nw_ref, wq_ref, k_ref, v_ref, wmo_ref, o_ref):
    h = (x_ref[0]
         + jnp.dot(yr_ref[0], wo_ref[:RWKV_WIDTH, :].astype(BF16), preferred_element_type=F32)
         + jnp.dot(yd_ref[0], wo_ref[RWKV_WIDTH:, :].astype(BF16), preferred_element_type=F32))
    hn = _rms(h, nw_ref[...]).astype(BF16)
    q = jnp.dot(hn, wq_ref[...].astype(BF16), preferred_element_type=F32) * (MEM_HEAD_DIM ** -0.5)
    q = q.astype(BF16)
    outs = []
    for hd in range(MEM_HEADS):
        sl = slice(hd * MEM_HEAD_DIM, (hd + 1) * MEM_HEAD_DIM)
        s = lax.dot_general(q[:, sl], k_ref[0, :, sl], (((1,), (1,)), ((), ())),
                            preferred_element_type=F32)
        pr = jnp.exp(s - jnp.max(s, axis=-1, keepdims=True))
        den = jnp.sum(pr, axis=-1, keepdims=True)
        outs.append(jnp.dot(pr.astype(BF16), v_ref[0, :, sl], preferred_element_type=F32) / den)
    o = jnp.concatenate(outs, axis=1).astype(BF16)
    o_ref[0] = h + jnp.dot(o, wmo_ref[...].astype(BF16), preferred_element_type=F32)


def _mix_mem(x, y_rwkv, y_diff, w_out, norm_w, w_mq, k_mem, v_mem, w_mo):
    B, S, D = x.shape
    M = k_mem.shape[1]
    wspec = _const_spec((D, D))
    rows = lambda w: pl.BlockSpec((1, ROW_TILE, w), lambda b, i: (b, i, 0))
    memspec = pl.BlockSpec((1, M, D), lambda b, i: (b, 0, 0))
    return pl.pallas_call(
        _mix_mem_kernel,
        grid=(B, S // ROW_TILE),
        in_specs=[rows(D), rows(RWKV_WIDTH), rows(DIFF_WIDTH), wspec, _const_spec((1, D)), wspec,
                  memspec, memspec, wspec],
        out_specs=rows(D),
        out_shape=jax.ShapeDtypeStruct((B, S, D), F32),
        compiler_params=_params("parallel", "parallel"),
        name="mix_mem",
    )(x, y_rwkv, y_diff, w_out, norm_w, w_mq, k_mem, v_mem, w_mo)


def _mlp_kernel(final_norm, h_ref, nw_ref, wu_ref, wd_ref, fw_ref, o_ref):
    h = h_ref[...]
    hn = _rms(h, nw_ref[...]).astype(BF16)
    acc = h
    for c in range(D_FF // FF_TILE):
        sl = slice(c * FF_TILE, (c + 1) * FF_TILE)
        u = jnp.maximum(jnp.dot(hn, wu_ref[:, sl].astype(BF16), preferred_element_type=F32), 0.0)
        acc = acc + jnp.dot((u * u).astype(BF16), wd_ref[sl, :].astype(BF16), preferred_element_type=F32)
    o_ref[...] = _rms(acc, fw_ref[...]) if final_norm else acc


def _mlp(h2, norm_w, w_up, w_down, final_w, final_norm):
    n, D = h2.shape
    rows = pl.BlockSpec((MLP_ROW_TILE, D), lambda i: (i, 0))
    return pl.pallas_call(
        functools.partial(_mlp_kernel, final_norm),
        grid=(n // MLP_ROW_TILE,),
        in_specs=[rows, _const_spec((1, D)), _const_spec((D, D_FF)), _const_spec((D_FF, D)),
                  _const_spec((1, D))],
        out_specs=rows,
        out_shape=jax.ShapeDtypeStruct((n, D), F32),
        compiler_params=_params("parallel"),
        name="mlp",
    )(h2, norm_w, w_up, w_down, final_w)


def kernel(x, mem, norm_mix_w, w_in, mu_shift, w_decay0, w_decay_up, a0, a_up, g_up, k_k, k_a, r_k,
           lnx_w, lnx_b, lam_q1, lam_k1, lam_q2, lam_k2, subln_w, w_out, norm_mem_w, norm_src_w,
           w_mq, w_mk, w_mv, w_mo, norm_mlp_w, w_up, w_down, norm_final_w):
    B, S, D = x.shape
    depth = norm_mix_w.shape[0]
    row = lambda t: t.reshape(1, -1)
    h = x
    for l in range(depth):
        lam_init = 0.8 - 0.6 * math.exp(-0.3 * l)
        zeros_lora = jnp.zeros((DECAY_LORA, RWKV_WIDTH), BF16)
        wdec_pad = jnp.concatenate([w_decay_up[l].astype(BF16), zeros_lora], axis=0)
        aup_pad = jnp.concatenate([zeros_lora, a_up[l].astype(BF16)], axis=0)

        n_cols = RWKV_COLS + 2 * DIFF_WIDTH
        w_qv_t = jnp.concatenate([w_in[l][:, RWKV_COLS:K_COL0], w_in[l][:, n_cols:]], axis=1).T
        p_rwkv, k_att, q_t, v_t = _in_proj(h.reshape(B * S, D), row(norm_mix_w[l]),
                                           w_in[l][:, :n_cols].astype(BF16), w_qv_t.astype(BF16),
                                           row(mu_shift[l]), B)
        y_rwkv = _rwkv(p_rwkv.reshape(B, S, RWKV_COLS), row(w_decay0[l]),
                       wdec_pad, row(a0[l]), aup_pad, g_up[l].astype(BF16), row(k_k[l]),
                       row(k_a[l]), row(r_k[l]), row(lnx_w[l]), row(lnx_b[l]))
        y_diff = _diff_attn(q_t, k_att.reshape(B, S, DIFF_WIDTH), v_t, row(lam_q1[l]), row(lam_k1[l]),
                            row(lam_q2[l]), row(lam_k2[l]), subln_w[l].reshape(-1, 1), lam_init)
        k_mem, v_mem = _mem_kv(mem.reshape(-1, D), row(norm_src_w[l]), w_mk[l], w_mv[l])
        M = mem.shape[1]
        h = _mix_mem(h, y_rwkv, y_diff, w_out[l], row(norm_mem_w[l]), w_mq[l],
                     k_mem.reshape(B, M, D), v_mem.reshape(B, M, D), w_mo[l])
        h = _mlp(h.reshape(B * S, D), row(norm_mlp_w[l]), w_up[l], w_down[l], row(norm_final_w),
                 l == depth - 1)
        h = h.reshape(B, S, D)
    return h
```

```python
import functools
import math

import jax
import jax.numpy as jnp
from jax import lax
from jax.experimental import pallas as pl
from jax.experimental.pallas import tpu as pltpu

F32 = jnp.float32
BF16 = jnp.bfloat16

D_MODEL = 1024
CHUNK = 64
RWKV_HEAD = 64
RWKV_WIDTH = 512
RWKV_HEADS = 8
DECAY_LORA = 64
AAA_LORA = 64
GATE_LORA = 128
RWKV_COLS = 3 * RWKV_WIDTH + DECAY_LORA + AAA_LORA + GATE_LORA
GN_EPS = 64e-5
DIFF_WIDTH = 512
DIFF_HEADS = 4
DIFF_VDIM = 128
DIFF_QKDIM = 64
DIFF_COLS = 3 * DIFF_WIDTH
D_IN_TOTAL = RWKV_COLS + DIFF_COLS
MEM_HEADS = 4
MEM_HEAD_DIM = 256
D_FF = 4 * D_MODEL
RMS_EPS = 1e-5
NEG_INF = -1e30
LOG2_E = 1.4426950408889634

VMEM_LIMIT_BYTES = 56 * 1024 * 1024

ROW_TILE = 1024
MLP_ROW_TILE = 512
ATTN_TILE = 512
FF_TILE = 1024
ATTN_HEADS_PER_STEP = 2


def _mm(a, b):
    return jnp.dot(a.astype(BF16), b.astype(BF16), preferred_element_type=F32)


def _mm_nt(a, b):
    return lax.dot_general(a.astype(BF16), b.astype(BF16), (((1,), (1,)), ((), ())),
                           preferred_element_type=F32)


def _mm_tn(a, b):
    return lax.dot_general(a.astype(BF16), b.astype(BF16), (((0,), (0,)), ((), ())),
                           preferred_element_type=F32)


def _rms(x, w, eps=RMS_EPS):
    return x * lax.rsqrt(jnp.mean(x * x, axis=-1, keepdims=True) + eps) * w


def _params(*sem, flags=None):
    return pltpu.CompilerParams(dimension_semantics=sem, vmem_limit_bytes=VMEM_LIMIT_BYTES,
                                flags=flags)


def _interleave(*gens):
    live = list(gens)
    while live:
        for gen in list(live):
            if next(gen, StopIteration) is StopIteration:
                live.remove(gen)


def _const_spec(shape):
    nd = len(shape)
    return pl.BlockSpec(shape, lambda *_: (0,) * nd, pipeline_mode=pl.Buffered(1))


K_COL0 = RWKV_COLS + DIFF_WIDTH


def _in_proj_kernel(tiles_per_seq, x_ref, nw_ref, w_ref, wqvt_ref, mu_ref, pr_ref, k_ref, qt_ref, vt_ref,
                    carry_ref):
    i = pl.program_id(0)
    rows = x_ref.shape[0]

    @pl.when(i == 0)
    def _():
        carry_ref[...] = jnp.zeros_like(carry_ref)

    xn = _rms(x_ref[...], nw_ref[...]).astype(BF16)
    p = jnp.dot(xn, w_ref[:, :RWKV_COLS], preferred_element_type=F32)
    last_prev = jnp.where(i % tiles_per_seq == 0, 0.0, carry_ref[...])
    first_row = lax.broadcasted_iota(jnp.int32, p.shape, 0) == 0
    p_prev = jnp.where(first_row, last_prev, pltpu.roll(p, 1, 0))
    carry_ref[...] = p[rows - 1:rows, :]
    pr_ref[...] = p + (p_prev - p) * mu_ref[...]
    k_ref[...] = jnp.dot(xn, w_ref[:, K_COL0:], preferred_element_type=F32).astype(BF16)
    qv_t = lax.dot_general(wqvt_ref[...], xn, (((1,), (1,)), ((), ())), preferred_element_type=F32)
    qt_ref[0, 0] = (qv_t[:DIFF_WIDTH] * (DIFF_QKDIM ** -0.5 * LOG2_E)).astype(BF16)
    vt_ref[0, 0] = qv_t[DIFF_WIDTH:].astype(BF16)


def _in_proj(x2, norm_w, w_in_bf, w_qvt_bf, mu, batch):
    n = x2.shape[0]
    tiles_per_seq = n // batch // ATTN_TILE
    n_cols = RWKV_COLS + 2 * DIFF_WIDTH
    t_spec = pl.BlockSpec((1, 1, DIFF_WIDTH, ATTN_TILE),
                          lambda i: (i // tiles_per_seq, i % tiles_per_seq, 0, 0))
    t_shape = jax.ShapeDtypeStruct((batch, tiles_per_seq, DIFF_WIDTH, ATTN_TILE), BF16)
    return pl.pallas_call(
        functools.partial(_in_proj_kernel, tiles_per_seq),
        grid=(n // ATTN_TILE,),
        in_specs=[pl.BlockSpec((ATTN_TILE, D_MODEL), lambda i: (i, 0)),
                  _const_spec((1, D_MODEL)),
                  _const_spec((D_MODEL, n_cols)),
                  _const_spec((2 * DIFF_WIDTH, D_MODEL)),
                  _const_spec((1, RWKV_COLS))],
        out_specs=[pl.BlockSpec((ATTN_TILE, RWKV_COLS), lambda i: (i, 0)),
                   pl.BlockSpec((ATTN_TILE, DIFF_WIDTH), lambda i: (i, 0)),
                   t_spec, t_spec],
        out_shape=[jax.ShapeDtypeStruct((n, RWKV_COLS), F32),
                   jax.ShapeDtypeStruct((n, DIFF_WIDTH), BF16),
                   t_shape, t_shape],
        scratch_shapes=[pltpu.VMEM((1, RWKV_COLS), F32)],
        compiler_params=_params("arbitrary"),
        name="in_proj",
    )(x2, norm_w, w_in_bf, w_qvt_bf, mu)


GROUP_LANES = 256
HEADS_PER_GROUP = GROUP_LANES // RWKV_HEAD
N_GROUPS = RWKV_WIDTH // GROUP_LANES
LANES = 128
RWKV_PREP_BF16 = ("kd", "rd", "bi", "ki", "bw", "kw", "v")
RWKV_PREP_F32 = ("bonus", "gate")


def _rwkv_tile_masks():
    r = jnp.arange(CHUNK)[:, None]
    c = (jnp.arange(GROUP_LANES) % CHUNK)[None, :]
    ms = [r > c, r >= c, r == c, (r > c) & ((r // 8) == (c // 8))]
    size = 8
    while size < CHUNK:
        ms.append(((r // (2 * size)) == (c // (2 * size))) & ((r // size) > (c // size)))
        size *= 2
    return jnp.stack(ms).astype(F32)


def _rwkv_kernel(x0_ref, xa_ref, xb_ref, w0_ref, wdec_ref, a0_ref, aup_ref, gup_ref, kk_ref, ka_ref,
                 rk_ref, lnw_ref, lnb_ref, seg_ref, tril_ref, tmask_ref, o_ref, st_ref, *prep_refs):
    T, W, G = CHUNK, RWKV_WIDTH, GROUP_LANES
    hb = xa_ref.shape[0]
    n_names = len(RWKV_PREP_BF16) + len(RWKV_PREP_F32) + 1
    names = RWKV_PREP_BF16 + RWKV_PREP_F32 + ("e_last",)
    stage_a = dict(zip(names, prep_refs[:n_names]))
    stage_b = dict(zip(names, prep_refs[n_names:]))

    seg = seg_ref[...]

    def head_sum(x):
        xb = x.astype(BF16)
        return jnp.concatenate(
            [jnp.dot(xb[:, g * G:(g + 1) * G], seg, preferred_element_type=F32)
             for g in range(N_GROUPS)], axis=1)

    def prepare(x_ref, stage):
        xx = jnp.concatenate([x_ref[b] for b in range(hb)], axis=0)
        r = xx[:, 0:W]
        k = xx[:, W:2 * W]
        v = xx[:, 2 * W:3 * W]
        lora_da = xx[:, 3 * W:3 * W + DECAY_LORA + AAA_LORA]
        gd = xx[:, 3 * W + DECAY_LORA + AAA_LORA:]

        z = w0_ref[...] + _mm(jnp.tanh(lora_da), wdec_ref[...])
        yield
        softplus_neg = jnp.maximum(-z, 0.0) + jnp.log(1.0 + jnp.exp(-jnp.abs(z)))
        yield
        log_decay = -jnp.exp(-softplus_neg - 0.5)
        yield
        a = jax.nn.sigmoid(a0_ref[...] + _mm(lora_da, aup_ref[...]))
        yield
        stage["gate"][...] = _mm(jax.nn.sigmoid(gd), gup_ref[...])
        yield

        kk = k * kk_ref[...]
        kk = kk * lax.rsqrt(jnp.maximum(head_sum(kk * kk), 1e-24))
        yield
        k_mod = k * (1.0 + (a - 1.0) * ka_ref[...])
        b_vec = kk * a
        yield

        ld_hi = log_decay.astype(BF16)
        ld_lo = (log_decay - ld_hi.astype(F32)).astype(BF16)
        tril = tril_ref[...]
        cl = (jnp.dot(tril, ld_hi, preferred_element_type=F32)
              + jnp.dot(tril, ld_lo, preferred_element_type=F32))
        yield
        e_incl = jnp.exp(cl)
        yield
        e_inv = jnp.exp(-cl)
        yield
        bi = b_vec * e_inv
        stage["bi"][...] = bi.astype(BF16)
        yield
        ki = k_mod * e_inv
        stage["ki"][...] = ki.astype(BF16)
        yield
        stage["kd"][...] = (kk * jnp.exp(cl - log_decay)).astype(BF16)
        yield
        stage["rd"][...] = (r * e_incl).astype(BF16)
        stage["v"][...] = v.astype(BF16)
        yield
        stage["bonus"][...] = head_sum(r * k_mod * rk_ref[...]) * v
        yield
        for b in range(hb):
            rows = slice(b * T, (b + 1) * T)
            w_total = e_incl[b * T + T - 1:b * T + T, :]
            stage["bw"][rows, :] = (bi[rows] * w_total).astype(BF16)
            stage["kw"][rows, :] = (ki[rows] * w_total).astype(BF16)
            stage["e_last"][b] = jnp.broadcast_to(w_total, (8, W))
            if b % 2 == 1:
                yield

    bdm32 = seg.astype(F32)
    m_strict, m_incl, m_eye, m_blk8 = (tmask_ref[i] for i in range(4))
    m_levels = [tmask_ref[i] for i in range(4, tmask_ref.shape[0])]
    m_both = jnp.concatenate([m_strict, m_incl], axis=0)

    lane_in_vreg = lax.broadcasted_iota(jnp.int32, (T, G), 1) % LANES
    half_masks = [jnp.where((lane_in_vreg // RWKV_HEAD) == j, 1.0, 0.0).astype(BF16)
                  for j in range(LANES // RWKV_HEAD)]
    zero_vreg_cols = jnp.zeros((T, LANES), BF16)

    def bd(y):
        yb = y.astype(BF16)
        kept = [yb * m for m in half_masks]
        rows = []
        for h in range(HEADS_PER_GROUP):
            col, j = divmod(h * RWKV_HEAD, LANES)
            j //= RWKV_HEAD
            rows.append(jnp.concatenate(
                [kept[j][:, c * LANES:(c + 1) * LANES] if c == col else zero_vreg_cols
                 for c in range(G // LANES)], axis=1))
        return jnp.concatenate(rows, axis=0)

    def recurrence(stage, b0):
        items = [(b, g) for b in range(hb) for g in range(N_GROUPS)]
        n_it = range(len(items))
        tile = lambda name, b, g: stage[name][b * T:(b + 1) * T, g * G:(g + 1) * G]
        kd = [tile("kd", b, g) for b, g in items]
        bi = [tile("bi", b, g) for b, g in items]
        ki = [tile("ki", b, g) for b, g in items]
        vv = [tile("v", b, g) for b, g in items]

        w_col = {}
        for b in range(hb):
            col = jnp.broadcast_to(stage["e_last"][b][0:1, :], (LANES, W)).T
            for g in range(N_GROUPS):
                w_col[(b, g)] = jnp.concatenate([col[g * G:(g + 1) * G]] * (G // LANES), axis=1)

        lhs = [jnp.concatenate([kd[i], tile("rd", *items[i])], axis=0) for i in n_it]
        a1 = [_mm(lhs[i], bd(bi[i]).astype(F32).T) for i in n_it]
        yield
        a2 = [_mm(lhs[i], bd(ki[i]).astype(F32).T) for i in n_it]
        yield
        a_k = [a2[i] * m_both for i in n_it]
        a_rb = [a1[i][T:] * m_incl for i in n_it]
        yield

        n0 = [a1[i][:T] * m_blk8 for i in n_it]
        n2 = [_mm(n0[i], bd(n0[i])) for i in n_it]
        yield
        n4 = [_mm(n2[i], bd(n2[i])) for i in n_it]
        yield
        x = [_mm(m_eye - n0[i], bd(m_eye + n2[i])) for i in n_it]
        yield
        x = [_mm(x[i], bd(m_eye + n4[i])) for i in n_it]
        yield
        for m_off in m_levels:
            t1 = [_mm(x[i], bd(a1[i][:T] * m_off)) for i in n_it]
            yield
            x = [x[i] - _mm(t1[i], bd(x[i])) for i in n_it]
            yield

        st = [st_ref[(b0 + b) * N_GROUPS + g] for b, g in items]
        ks = [jnp.dot(lhs[i], st[i].astype(BF16), preferred_element_type=F32) for i in n_it]
        yield
        sv = [ks[i] + _mm(a_k[i], bd(vv[i])) for i in n_it]
        yield
        u = [-_mm(x[i], bd(sv[i][:T])) for i in n_it]
        yield
        y_tiles = [sv[i][T:] + _mm(a_rb[i], bd(u[i])) for i in n_it]
        yield
        for i, (b, g) in enumerate(items):
            wk = jnp.concatenate([tile("bw", b, g), tile("kw", b, g)], axis=0)
            grow = _mm_tn(wk, jnp.concatenate([u[i].astype(BF16), vv[i]], axis=0)) * bdm32
            st_ref[(b0 + b) * N_GROUPS + g] = st[i] * w_col[(b, g)] + grow
            if i % 4 == 3:
                yield

        y = jnp.concatenate(
            [jnp.concatenate(y_tiles[b * N_GROUPS:(b + 1) * N_GROUPS], axis=1) for b in range(hb)],
            axis=0)
        inv_n = 1.0 / RWKV_HEAD
        mean = head_sum(y) * inv_n
        d = y - mean
        var = head_sum(d * d) * inv_n
        yield
        yn = d * lax.rsqrt(var + GN_EPS) * lnw_ref[...] + lnb_ref[...]
        out = ((yn + stage["bonus"][...]) * stage["gate"][...]).astype(o_ref.dtype)
        for b in range(hb):
            o_ref[b0 + b] = out[b * T:(b + 1) * T]

    @pl.when(pl.program_id(0) == 0)
    def _():
        st_ref[...] = jnp.zeros_like(st_ref)
        _interleave(prepare(x0_ref, stage_a))

    _interleave(recurrence(stage_a, 0), prepare(xb_ref, stage_b))
    _interleave(recurrence(stage_b, hb), prepare(xa_ref, stage_a))


def _rwkv(xx, w0, wdec_pad, a0, aup_pad, gup, k_k, k_a, r_k, lnx_w, lnx_b):
    B, S, _ = xx.shape
    W, G = RWKV_WIDTH, GROUP_LANES
    hb = B // 2
    n_chunks = S // CHUNK
    head = jnp.arange(G) // RWKV_HEAD
    seg = (head[:, None] == head[None, :]).astype(BF16)
    t = jnp.arange(hb * CHUNK)
    tril = ((t[:, None] >= t[None, :]) & (t[:, None] // CHUNK == t[None, :] // CHUNK)).astype(BF16)
    tmasks = _rwkv_tile_masks()
    vec = lambda n: _const_spec((1, n))
    half = lambda dtype: pltpu.VMEM((hb * CHUNK, W), dtype)
    stage = ([half(BF16)] * len(RWKV_PREP_BF16) + [half(F32)] * len(RWKV_PREP_F32)
             + [pltpu.VMEM((hb, 8, W), F32)])
    return pl.pallas_call(
        _rwkv_kernel,
        grid=(n_chunks,),
        in_specs=[pl.BlockSpec((hb, CHUNK, RWKV_COLS), lambda j: (0, 0, 0)),
                  pl.BlockSpec((hb, CHUNK, RWKV_COLS), lambda j: (0, jnp.minimum(j + 1, n_chunks - 1), 0)),
                  pl.BlockSpec((hb, CHUNK, RWKV_COLS), lambda j: (1, j, 0)),
                  vec(W), _const_spec((LANES, W)), vec(W), _const_spec((LANES, W)),
                  _const_spec((GATE_LORA, W)), vec(W), vec(W), vec(W), vec(W), vec(W),
                  _const_spec((G, G)), _const_spec((hb * CHUNK, hb * CHUNK)),
                  _const_spec(tmasks.shape)],
        out_specs=pl.BlockSpec((B, CHUNK, W), lambda j: (0, j, 0)),
        out_shape=jax.ShapeDtypeStruct((B, S, W), BF16),
        scratch_shapes=[pltpu.VMEM((B * N_GROUPS, G, G), F32)] + stage + stage,
        compiler_params=_params("arbitrary"),
        name="rwkv7",
    )(xx, xx, xx, w0, wdec_pad, a0, aup_pad, gup, k_k, k_a, r_k, lnx_w, lnx_b, seg, tril, tmasks)


def _diff_attn_kernel(lam_init, qt_in_ref, k_ref, vt_ref, lq1_ref, lk1_ref, lq2_ref, lk2_ref, sw_ref,
                      o_ref, qt_ref, m_ref, l_ref, acc_ref, sa_ref, sb_ref, ca_ref, cb_ref):
    tq = tk = ATTN_TILE
    D, H = DIFF_VDIM, ATTN_HEADS_PER_STEP
    hs = range(H)
    i = pl.program_id(2)

    d_row = lax.broadcasted_iota(jnp.int32, (D, tq), 0)
    for h in hs:
        qt = qt_in_ref[0, 0, h * D:(h + 1) * D, :]
        zero = jnp.zeros_like(qt)
        qt_ref[h] = jnp.concatenate([jnp.where(d_row < DIFF_QKDIM, qt, zero),
                                     jnp.where(d_row >= DIFF_QKDIM, qt, zero)], axis=1)

    m_ref[...] = jnp.full(m_ref.shape, NEG_INF, F32)
    l_ref[...] = jnp.zeros(l_ref.shape, F32)
    acc_ref[...] = jnp.zeros(acc_ref.shape, F32)

    n_kc = tk // CHUNK

    def produce(t, buf):
        s_ref, cmax_ref = buf
        off = pl.multiple_of(t * tk, tk)
        for h in hs:
            s = jnp.dot(k_ref[0, pl.ds(off, tk), h * D:(h + 1) * D], qt_ref[h],
                        preferred_element_type=F32)
            s_ref[h] = s
            cmax_ref[h] = jnp.max(s.reshape(n_kc, CHUNK, 2 * tq), axis=1)

    def produce_diagonal(t, buf):
        s_ref, cmax_ref = buf
        off = pl.multiple_of(t * tk, tk)
        hk, hq = tk // 2, tq // 2
        for h in hs:
            top = jnp.dot(k_ref[0, pl.ds(off, hk), h * D:(h + 1) * D], qt_ref[h],
                          preferred_element_type=F32)
            s_ref[h, :hk, :] = top
            cmax_ref[h, :n_kc // 2, :] = jnp.max(top.reshape(n_kc // 2, CHUNK, 2 * tq), axis=1)
            for map_off in (0, tq):
                lanes = slice(map_off + hq, map_off + tq)
                bot = jnp.dot(k_ref[0, pl.ds(off + hk, hk), h * D:(h + 1) * D], qt_ref[h, :, lanes],
                              preferred_element_type=F32)
                s_ref[h, hk:, lanes] = bot
                cmax_ref[h, n_kc // 2:, lanes] = jnp.max(bot.reshape(n_kc // 2, CHUNK, hq), axis=1)

    def consume(t, buf):
        s_ref, cmax_ref = buf
        s = [s_ref[h] for h in hs]
        m_old = [m_ref[h] for h in hs]
        m_new = [jnp.maximum(m_old[h], jnp.max(cmax_ref[h], axis=0, keepdims=True)) for h in hs]
        alpha = [jnp.exp2(m_old[h] - m_new[h]) for h in hs]
        p = [jnp.exp2(s[h] - m_new[h]) for h in hs]
        pv = [jnp.dot(vt_ref[0, t, h * D:(h + 1) * D, :], p[h].astype(BF16),
                      preferred_element_type=F32) for h in hs]
        for h in hs:
            m_ref[h] = m_new[h]
            l_ref[h] = alpha[h] * l_ref[h] + jnp.sum(p[h], axis=0, keepdims=True)
            acc_ref[h] = alpha[h] * acc_ref[h] + pv[h]

    def consume_diagonal(t, buf):
        s_ref, cmax_ref = buf
        half_q = tq // 2
        units = []
        for h in hs:
            for map_off in (0, tq):
                units.append((h, map_off, tk // 2, 0))
                units.append((h, map_off + half_q, tk, half_q))
        ids = range(len(units))

        def masks(kext, q0):
            qry_chunk = lambda shape: (lax.broadcasted_iota(jnp.int32, shape, 1) + q0) // CHUNK
            chunk_ok = lax.broadcasted_iota(jnp.int32, (kext // CHUNK, half_q), 0) <= qry_chunk((kext // CHUNK, half_q))
            key_ok = (lax.broadcasted_iota(jnp.int32, (kext, half_q), 0) // CHUNK) <= qry_chunk((kext, half_q))
            return chunk_ok, key_ok

        mask_of = {(kext, q0): masks(kext, q0) for _, _, kext, q0 in units}
        lanes = [slice(lo, lo + half_q) for _, lo, _, _ in units]
        cmax = [jnp.where(mask_of[(kext, q0)][0], cmax_ref[h][:kext // CHUNK, lanes[u]], NEG_INF)
                for u, (h, lo, kext, q0) in enumerate(units)]
        s = [jnp.where(mask_of[(kext, q0)][1], s_ref[h][:kext, lanes[u]], NEG_INF)
             for u, (h, lo, kext, q0) in enumerate(units)]
        m_old = [m_ref[units[u][0]][:, lanes[u]] for u in ids]
        m_new = [jnp.maximum(m_old[u], jnp.max(cmax[u], axis=0, keepdims=True)) for u in ids]
        alpha = [jnp.exp2(m_old[u] - m_new[u]) for u in ids]
        p = [jnp.exp2(s[u] - m_new[u]) for u in ids]
        pv = [jnp.dot(vt_ref[0, t, units[u][0] * D:(units[u][0] + 1) * D, :units[u][2]], p[u].astype(BF16),
                      preferred_element_type=F32) for u in ids]
        for u in ids:
            h = units[u][0]
            m_ref[h, :, lanes[u]] = m_new[u]
            l_ref[h, :, lanes[u]] = alpha[u] * l_ref[h, :, lanes[u]] + jnp.sum(p[u], axis=0, keepdims=True)
            acc_ref[h, :, lanes[u]] = alpha[u] * acc_ref[h, :, lanes[u]] + pv[u]

    def finish(buf):
        consume_diagonal(i, buf)
        lam = (jnp.exp(jnp.sum(lq1_ref[...] * lk1_ref[...], axis=-1, keepdims=True))
               - jnp.exp(jnp.sum(lq2_ref[...] * lk2_ref[...], axis=-1, keepdims=True)) + lam_init)
        outs = []
        for h in hs:
            o2 = acc_ref[h] / l_ref[h]
            o = o2[:, :tq] - lam * o2[:, tq:]
            o = o * lax.rsqrt(jnp.mean(o * o, axis=0, keepdims=True) + RMS_EPS) * sw_ref[...]
            outs.append((o * (1.0 - lam_init)).T)
        o_ref[0] = jnp.concatenate(outs, axis=1).astype(o_ref.dtype)

    buf_a, buf_b = (sa_ref, ca_ref), (sb_ref, cb_ref)
    produce(0, buf_a)

    def pair(pi, carry):
        t = 2 * pi
        produce(t + 1, buf_b)
        consume(t, buf_a)
        produce(t + 2, buf_a)
        consume(t + 1, buf_b)
        return carry

    lax.fori_loop(0, i // 2, pair, 0)

    @pl.when(i % 2 == 1)
    def _():
        produce_diagonal(i, buf_b)
        consume(i - 1, buf_a)
        finish(buf_b)

    @pl.when(i % 2 == 0)
    def _():
        finish(buf_a)


def _diff_attn(q_t, k, v_t, lam_q1, lam_k1, lam_q2, lam_k2, subln_col, lam_init):
    B, S, _ = k.shape
    H, D, tq = ATTN_HEADS_PER_STEP, DIFF_VDIM, ATTN_TILE
    hw = H * D
    lamv = _const_spec((1, DIFF_QKDIM))
    scores = pltpu.VMEM((H, tq, 2 * tq), F32)
    chunk_max = pltpu.VMEM((H, tq // CHUNK, 2 * tq), F32)
    return pl.pallas_call(
        functools.partial(_diff_attn_kernel, lam_init),
        grid=(B, DIFF_HEADS // H, S // tq),
        in_specs=[pl.BlockSpec((1, 1, hw, tq), lambda b, g, i: (b, i, g, 0)),
                  pl.BlockSpec((1, S, hw), lambda b, g, i: (b, 0, g)),
                  pl.BlockSpec((1, S // tq, hw, tq), lambda b, g, i: (b, 0, g, 0)),
                  lamv, lamv, lamv, lamv, _const_spec((D, 1))],
        out_specs=pl.BlockSpec((1, tq, hw), lambda b, g, i: (b, i, g)),
        out_shape=jax.ShapeDtypeStruct((B, S, DIFF_WIDTH), BF16),
        scratch_shapes=[pltpu.VMEM((H, D, 2 * tq), BF16),
                        pltpu.VMEM((H, 1, 2 * tq), F32), pltpu.VMEM((H, 1, 2 * tq), F32),
                        pltpu.VMEM((H, D, 2 * tq), F32), scores, scores, chunk_max, chunk_max],
        compiler_params=_params("parallel", "parallel", "arbitrary"),
        name="diff_attn",
    )(q_t, k, v_t, lam_q1, lam_k1, lam_q2, lam_k2, subln_col)


def _mem_kv_kernel(m_ref, nw_ref, wk_ref, wv_ref, k_ref, v_ref):
    mn = _rms(m_ref[...], nw_ref[...]).astype(BF16)
    k_ref[...] = jnp.dot(mn, wk_ref[...].astype(BF16), preferred_element_type=F32).astype(BF16)
    v_ref[...] = jnp.dot(mn, wv_ref[...].astype(BF16), preferred_element_type=F32).astype(BF16)


def _mem_kv(mem2, norm_w, w_mk, w_mv):
    n = mem2.shape[0]
    wspec = _const_spec((D_MODEL, D_MODEL))
    rows = pl.BlockSpec((ROW_TILE, D_MODEL), lambda i: (i, 0))
    return pl.pallas_call(
        _mem_kv_kernel,
        grid=(n // ROW_TILE,),
        in_specs=[rows, _const_spec((1, D_MODEL)), wspec, wspec],
        out_specs=[rows, rows],
        out_shape=[jax.ShapeDtypeStruct((n, D_MODEL), BF16)] * 2,
        compiler_params=_params("parallel"),
        name="mem_kv",
    )(mem2, norm_w, w_mk, w_mv)


def _mix_mem_kernel(x_ref, yr_ref, yd_ref, wo_ref, nw_ref, wq_ref, k_ref, v_ref, wmo_ref, o_ref):
    h = (x_ref[0]
         + jnp.dot(yr_ref[0], wo_ref[:RWKV_WIDTH, :].astype(BF16), preferred_element_type=F32)
         + jnp.dot(yd_ref[0], wo_ref[RWKV_WIDTH:, :].astype(BF16), preferred_element_type=F32))
    hn = _rms(h, nw_ref[...]).astype(BF16)
    q = jnp.dot(hn, wq_ref[...].astype(BF16), preferred_element_type=F32) * (MEM_HEAD_DIM ** -0.5)
    q = q.astype(BF16)
    outs = []
    for hd in range(MEM_HEADS):
        sl = slice(hd * MEM_HEAD_DIM, (hd + 1) * MEM_HEAD_DIM)
        s = lax.dot_general(q[:, sl], k_ref[0, :, sl], (((1,), (1,)), ((), ())),
                            preferred_element_type=F32)
        pr = jnp.exp(s - jnp.max(s, axis=-1, keepdims=True))
        den = jnp.sum(pr, axis=-1, keepdims=True)
        outs.append(jnp.dot(pr.astype(BF16), v_ref[0, :, sl], preferred_element_type=F32) / den)
    o = jnp.concatenate(outs, axis=1).astype(BF16)
    o_ref[0] = h + jnp.dot(o, wmo_ref[...].astype(BF16), preferred_element_type=F32)


def _mix_mem(x, y_rwkv, y_diff, w_out, norm_w, w_mq, k_mem, v_mem, w_mo):
    B, S, D = x.shape
    M = k_mem.shape[1]
    wspec = _const_spec((D, D))
    rows = lambda w: pl.BlockSpec((1, ROW_TILE, w), lambda b, i: (b, i, 0))
    memspec = pl.BlockSpec((1, M, D), lambda b, i: (b, 0, 0))
    return pl.pallas_call(
        _mix_mem_kernel,
        grid=(B, S // ROW_TILE),
        in_specs=[rows(D), rows(RWKV_WIDTH), rows(DIFF_WIDTH), wspec, _const_spec((1, D)), wspec,
                  memspec, memspec, wspec],
        out_specs=rows(D),
        out_shape=jax.ShapeDtypeStruct((B, S, D), F32),
        compiler_params=_params("parallel", "parallel"),
        name="mix_mem",
    )(x, y_rwkv, y_diff, w_out, norm_w, w_mq, k_mem, v_mem, w_mo)


def _mlp_kernel(final_norm, h_ref, nw_ref, wu_ref, wd_ref, fw_ref, o_ref):
    h = h_ref[...]
    hn = _rms(h, nw_ref[...]).astype(BF16)
    acc = h
    for c in range(D_FF // FF_TILE):
        sl = slice(c * FF_TILE, (c + 1) * FF_TILE)
        u = jnp.maximum(jnp.dot(hn, wu_ref[:, sl].astype(BF16), preferred_element_type=F32), 0.0)
        acc = acc + jnp.dot((u * u).astype(BF16), wd_ref[sl, :].astype(BF16), preferred_element_type=F32)
    o_ref[...] = _rms(acc, fw_ref[...]) if final_norm else acc


def _mlp(h2, norm_w, w_up, w_down, final_w, final_norm):
    n, D = h2.shape
    rows = pl.BlockSpec((MLP_ROW_TILE, D), lambda i: (i, 0))
    return pl.pallas_call(
        functools.partial(_mlp_kernel, final_norm),
        grid=(n // MLP_ROW_TILE,),
        in_specs=[rows, _const_spec((1, D)), _const_spec((D, D_FF)), _const_spec((D_FF, D)),
                  _const_spec((1, D))],
        out_specs=rows,
        out_shape=jax.ShapeDtypeStruct((n, D), F32),
        compiler_params=_params("parallel"),
        name="mlp",
    )(h2, norm_w, w_up, w_down, final_w)


def kernel(x, mem, norm_mix_w, w_in, mu_shift, w_decay0, w_decay_up, a0, a_up, g_up, k_k, k_a, r_k,
           lnx_w, lnx_b, lam_q1, lam_k1, lam_q2, lam_k2, subln_w, w_out, norm_mem_w, norm_src_w,
           w_mq, w_mk, w_mv, w_mo, norm_mlp_w, w_up, w_down, norm_final_w):
    B, S, D = x.shape
    depth = norm_mix_w.shape[0]
    row = lambda t: t.reshape(1, -1)
    h = x
    for l in range(depth):
        lam_init = 0.8 - 0.6 * math.exp(-0.3 * l)
        zeros_lora = jnp.zeros((DECAY_LORA, RWKV_WIDTH), BF16)
        wdec_pad = jnp.concatenate([w_decay_up[l].astype(BF16), zeros_lora], axis=0)
        aup_pad = jnp.concatenate([zeros_lora, a_up[l].astype(BF16)], axis=0)

        n_cols = RWKV_COLS + 2 * DIFF_WIDTH
        w_qv_t = jnp.concatenate([w_in[l][:, RWKV_COLS:K_COL0], w_in[l][:, n_cols:]], axis=1).T
        p_rwkv, k_att, q_t, v_t = _in_proj(h.reshape(B * S, D), row(norm_mix_w[l]),
                                           w_in[l][:, :n_cols].astype(BF16), w_qv_t.astype(BF16),
                                           row(mu_shift[l]), B)
        y_rwkv = _rwkv(p_rwkv.reshape(B, S, RWKV_COLS), row(w_decay0[l]),
                       wdec_pad, row(a0[l]), aup_pad, g_up[l].astype(BF16), row(k_k[l]),
                       row(k_a[l]), row(r_k[l]), row(lnx_w[l]), row(lnx_b[l]))
        y_diff = _diff_attn(q_t, k_att.reshape(B, S, DIFF_WIDTH), v_t, row(lam_q1[l]), row(lam_k1[l]),
                            row(lam_q2[l]), row(lam_k2[l]), subln_w[l].reshape(-1, 1), lam_init)
        k_mem, v_mem = _mem_kv(mem.reshape(-1, D), row(norm_src_w[l]), w_mk[l], w_mv[l])
        M = mem.shape[1]
        h = _mix_mem(h, y_rwkv, y_diff, w_out[l], row(norm_mem_w[l]), w_mq[l],
                     k_mem.reshape(B, M, D), v_mem.reshape(B, M, D), w_mo[l])
        h = _mlp(h.reshape(B * S, D), row(norm_mlp_w[l]), w_up[l], w_down[l], row(norm_final_w),
                 l == depth - 1)
        h = h.reshape(B, S, D)
    return h
```
